```python
import jax, jax.numpy as jnp
from jax import lax
import numpy as np

D_MODEL = 1024
BATCH = 8
SEQ = 8192
DEPTH = 2

GRID_W = 64
CTX_LEN = 256
HEAD_DIM = 128
N_Q_HEADS = 8
N_KV_HEADS = 2
Q_PER_KV = N_Q_HEADS // N_KV_HEADS
ATTN_WIDTH = N_Q_HEADS * HEAD_DIM
KV_WIDTH = N_KV_HEADS * HEAD_DIM
ROPE_THETA = 10000.0
ROPE_AXIS_DIM = HEAD_DIM // 2
ROPE_FREQS = ROPE_AXIS_DIM // 2
Q_BLOCK = 128
SC_WIDTH = D_MODEL
SC_KERNEL = 3
CF_WIDTH = D_MODEL
CF_KERNEL = 31
N_BRANCHES = 3
D_FF = 4 * D_MODEL
NORM_EPS = 1e-6
LN_EPS = 1e-5
OFF_Q = 0
OFF_K = OFF_Q + ATTN_WIDTH
OFF_V = OFF_K + KV_WIDTH
OFF_SC = OFF_V + KV_WIDTH
OFF_CF = OFF_SC + 3 * SC_WIDTH
OFF_GATE = OFF_CF + 2 * CF_WIDTH
D_IN = OFF_GATE + N_BRANCHES * D_MODEL

kernel_name = "hybrid_gated_attn_shortconv_conformer_dit"


def rmsnorm(x, g):
    xf = x.astype(jnp.float32)
    y = xf * lax.rsqrt(jnp.mean(xf * xf, axis=-1, keepdims=True) + NORM_EPS)
    return (y * g.astype(jnp.float32)).astype(x.dtype)


def layernorm(x, g, b):
    xf = x.astype(jnp.float32)
    mu = jnp.mean(xf, axis=-1, keepdims=True)
    xc = xf - mu
    y = xc * lax.rsqrt(jnp.mean(xc * xc, axis=-1, keepdims=True) + LN_EPS)
    return (y * g.astype(jnp.float32) + b.astype(jnp.float32)).astype(x.dtype)


def modulation(cond, w, b):
    m = jax.nn.silu(cond) @ w + b
    return jnp.split(m[:, None, :], 6, axis=-1)


def axial_rope_tables(n_tokens):
    rows = n_tokens // GRID_W
    row = jnp.repeat(jnp.arange(rows), GRID_W)
    col = jnp.tile(jnp.arange(GRID_W), rows)
    pos = jnp.stack([row, col], axis=-1).astype(jnp.float32)
    inv_freq = ROPE_THETA ** (-jnp.arange(ROPE_FREQS, dtype=jnp.float32) * 2.0 / ROPE_AXIS_DIM)
    ang = pos[:, :, None] * inv_freq
    return jnp.cos(ang), jnp.sin(ang)


def apply_rope(x, cos, sin):
    B, S, H, _ = x.shape
    xr = x.reshape(B, S, H, 2, 2, ROPE_FREQS).astype(jnp.float32)
    xa, xb = xr[..., 0, :], xr[..., 1, :]
    c = cos[None, :, None]
    s = sin[None, :, None]
    out = jnp.stack([xa * c - xb * s, xb * c + xa * s], axis=-2)
    return out.reshape(x.shape).astype(x.dtype)


def dwconv(x, w):
    return lax.conv_general_dilated(
        x, w[:, None, :], window_strides=(1,), padding='SAME',
        dimension_numbers=('NWC', 'WIO', 'NWC'), feature_group_count=x.shape[-1])


def heads(x, n):
    return x.reshape(*x.shape[:-1], n, HEAD_DIM)


def query_heads(p, q_gain):
    return rmsnorm(heads(p[..., OFF_Q:OFF_K], N_Q_HEADS), q_gain)


def keys_values(p_kv, k_gain):
    k = rmsnorm(heads(p_kv[..., :KV_WIDTH], N_KV_HEADS), k_gain)
    v = heads(p_kv[..., KV_WIDTH:], N_KV_HEADS)
    return k, v


def attend(q, k, v):
    s = jnp.einsum('bqhgd,bkhd->bhgqk', q, k).astype(jnp.float32) * (HEAD_DIM ** -0.5)
    p = jax.nn.softmax(s, axis=-1).astype(v.dtype)
    return jnp.einsum('bhgqk,bkhd->bqhgd', p, v)


def latent_attention(q, k, v, k_ctx, v_ctx):
    B, S = q.shape[:2]
    k_all = jnp.concatenate([k_ctx, k], axis=1)
    v_all = jnp.concatenate([v_ctx, v], axis=1)
    n_blk = S // Q_BLOCK
    qb = q.reshape(B, n_blk, Q_BLOCK, N_KV_HEADS, Q_PER_KV, HEAD_DIM).transpose(1, 0, 2, 3, 4, 5)
    out = lax.map(lambda qi: attend(qi, k_all, v_all), qb)
    return out.transpose(1, 0, 2, 3, 4, 5).reshape(B, S, ATTN_WIDTH)


def context_attention(q, k, v):
    B, L = q.shape[:2]
    qg = q.reshape(B, L, N_KV_HEADS, Q_PER_KV, HEAD_DIM)
    return attend(qg, k, v).reshape(B, L, ATTN_WIDTH)


def mixer_merge(p, attn, lp):
    y_attn = attn @ lp['w_attn_out']
    sc = p[..., OFF_SC:OFF_CF]
    b_gate, c_gate, h_sc = jnp.split(sc, 3, axis=-1)
    y_sc = (b_gate * dwconv(c_gate * h_sc, lp['w_sc_conv'])) @ lp['w_sc_out']
    cf_a, cf_g = jnp.split(p[..., OFF_CF:OFF_GATE], 2, axis=-1)
    u = cf_a * jax.nn.sigmoid(cf_g)
    u = dwconv(u, lp['w_cf_conv']) + lp['b_cf_conv']
    u = jax.nn.silu(layernorm(u, lp['g_cf_ln'], lp['b_cf_ln']))
    y_cf = u @ lp['w_cf_out'] + lp['b_cf_out']
    g_a, g_b, g_c = jnp.split(jax.nn.sigmoid(p[..., OFF_GATE:]), N_BRANCHES, axis=-1)
    merged = g_a * y_attn + g_b * y_sc + g_c * y_cf
    return merged @ lp['w_o']


def squared_relu_mlp(h, w1, w2):
    return jnp.square(jax.nn.relu(h @ w1)) @ w2


def setup_inputs(seed: int = 0) -> dict:
    key = jax.random.key(seed)
    ks = jax.random.split(key, 32)
    f32 = jnp.float32
    nrm = lambda k, shape, s: jax.random.normal(k, shape, f32) * s
    L = DEPTH
    return {
        "x": nrm(ks[0], (BATCH, SEQ, D_MODEL), 1.0),
        "c": nrm(ks[1], (BATCH, D_MODEL), 1.0),
        "ctx": nrm(ks[2], (BATCH, CTX_LEN, D_MODEL), 1.0),
        "c_ctx": nrm(ks[3], (D_MODEL,), 1.0),
        "w_mod": nrm(ks[4], (L, D_MODEL, 6 * D_MODEL), 0.5 * D_MODEL ** -0.5),
        "b_mod": nrm(ks[5], (L, 6 * D_MODEL), 0.01),
        "g_norm1": 1.0 + nrm(ks[6], (L, D_MODEL), 0.02),
        "g_norm2": 1.0 + nrm(ks[7], (L, D_MODEL), 0.02),
        "w_in": nrm(ks[8], (L, D_MODEL, D_IN), D_MODEL ** -0.5),
        "q_gain": 1.0 + nrm(ks[9], (L, HEAD_DIM), 0.02),
        "k_gain": 1.0 + nrm(ks[10], (L, HEAD_DIM), 0.02),
        "w_attn_out": nrm(ks[11], (L, ATTN_WIDTH, D_MODEL), ATTN_WIDTH ** -0.5),
        "w_sc_conv": nrm(ks[12], (L, SC_KERNEL, SC_WIDTH), SC_KERNEL ** -0.5),
        "w_sc_out": nrm(ks[13], (L, SC_WIDTH, D_MODEL), SC_WIDTH ** -0.5),
        "w_cf_conv": nrm(ks[14], (L, CF_KERNEL, CF_WIDTH), CF_KERNEL ** -0.5),
        "b_cf_conv": nrm(ks[15], (L, CF_WIDTH), 0.01),
        "g_cf_ln": 1.0 + nrm(ks[16], (L, CF_WIDTH), 0.02),
        "b_cf_ln": nrm(ks[17], (L, CF_WIDTH), 0.01),
        "w_cf_out": nrm(ks[18], (L, CF_WIDTH, D_MODEL), CF_WIDTH ** -0.5),
        "b_cf_out": nrm(ks[19], (L, D_MODEL), 0.01),
        "w_o": nrm(ks[20], (L, D_MODEL, D_MODEL), D_MODEL ** -0.5),
        "w_mlp_in": nrm(ks[21], (L, D_MODEL, D_FF), D_MODEL ** -0.5),
        "w_mlp_out": nrm(ks[22], (L, D_FF, D_MODEL), D_FF ** -0.5),
        "g_final": 1.0 + nrm(ks[23], (D_MODEL,), 0.02),
    }


def reference(x, c, ctx, c_ctx, w_mod, b_mod, g_norm1, g_norm2, w_in, q_gain, k_gain,
              w_attn_out, w_sc_conv, w_sc_out, w_cf_conv, b_cf_conv, g_cf_ln, b_cf_ln,
              w_cf_out, b_cf_out, w_o, w_mlp_in, w_mlp_out, g_final):
    cos, sin = axial_rope_tables(x.shape[1])
    for l in range(DEPTH):
        last = l == DEPTH - 1
        lp = dict(w_attn_out=w_attn_out[l], w_sc_conv=w_sc_conv[l], w_sc_out=w_sc_out[l],
                  w_cf_conv=w_cf_conv[l], b_cf_conv=b_cf_conv[l], g_cf_ln=g_cf_ln[l],
                  b_cf_ln=b_cf_ln[l], w_cf_out=w_cf_out[l], b_cf_out=b_cf_out[l], w_o=w_o[l])
        sh1, sc1, gt1, sh2, sc2, gt2 = modulation(c, w_mod[l], b_mod[l])
        csh1, csc1, cgt1, csh2, csc2, cgt2 = modulation(c_ctx[None, :], w_mod[l], b_mod[l])

        h_lat = rmsnorm(x, g_norm1[l]) * (1.0 + sc1) + sh1
        h_ctx = rmsnorm(ctx, g_norm1[l]) * (1.0 + csc1) + csh1
        p_lat = h_lat @ w_in[l]
        if last:
            p_ctx_kv = h_ctx @ w_in[l][:, OFF_K:OFF_SC]
        else:
            p_ctx = h_ctx @ w_in[l]
            p_ctx_kv = p_ctx[..., OFF_K:OFF_SC]
        k_ctx, v_ctx = keys_values(p_ctx_kv, k_gain[l])
        q_lat = apply_rope(query_heads(p_lat, q_gain[l]), cos, sin)
        k_lat, v_lat = keys_values(p_lat[..., OFF_K:OFF_SC], k_gain[l])
        k_lat = apply_rope(k_lat, cos, sin)
        attn_lat = latent_attention(q_lat, k_lat, v_lat, k_ctx, v_ctx)
        x_new = x + gt1 * mixer_merge(p_lat, attn_lat, lp)

        h2 = rmsnorm(x_new, g_norm2[l]) * (1.0 + sc2) + sh2
        x_new = x_new + gt2 * squared_relu_mlp(h2, w_mlp_in[l], w_mlp_out[l])

        if not last:
            attn_ctx = context_attention(query_heads(p_ctx, q_gain[l]), k_ctx, v_ctx)
            ctx = ctx + cgt1 * mixer_merge(p_ctx, attn_ctx, lp)
            hc2 = rmsnorm(ctx, g_norm2[l]) * (1.0 + csc2) + csh2
            ctx = ctx + cgt2 * squared_relu_mlp(hc2, w_mlp_in[l], w_mlp_out[l])
        x = x_new
    return rmsnorm(x, g_final)
```

```python
import functools

import jax
import jax.numpy as jnp
from jax import lax
from jax.experimental import pallas as pl
from jax.experimental.pallas import tpu as pltpu

D_MODEL = 1024
GRID_W = 64
HEAD_DIM = 128
N_Q_HEADS = 8
N_KV_HEADS = 2
Q_PER_KV = N_Q_HEADS // N_KV_HEADS
ATTN_WIDTH = N_Q_HEADS * HEAD_DIM
KV_WIDTH = N_KV_HEADS * HEAD_DIM
ROPE_THETA = 10000.0
ROPE_AXIS_DIM = HEAD_DIM // 2
ROPE_FREQS = ROPE_AXIS_DIM // 2
SC_KERNEL = 3
CF_KERNEL = 31
D_FF = 4 * D_MODEL
NORM_EPS = 1e-6
LN_EPS = 1e-5
OFF_Q = 0
OFF_K = OFF_Q + ATTN_WIDTH
OFF_V = OFF_K + KV_WIDTH
OFF_SC = OFF_V + KV_WIDTH
OFF_CF = OFF_SC + 3 * D_MODEL
OFF_GATE = OFF_CF + 2 * D_MODEL
D_IN = OFF_GATE + 3 * D_MODEL

V7X_VMEM_BYTES = 64 * 1024 * 1024
VMEM_LIMIT_BYTES = V7X_VMEM_BYTES - 8 * 1024 * 1024
SUBLANES = 8
HALO = 16
MOD_ROWS = 16
CONV_ROWS = 16

F32 = jnp.float32
BF16 = jnp.bfloat16


def _params(*sem):
    return pltpu.CompilerParams(dimension_semantics=sem, vmem_limit_bytes=VMEM_LIMIT_BYTES)


def _resident(shape):
    nd = len(shape)
    return pl.BlockSpec(shape, lambda *_: (0,) * nd, pipeline_mode=pl.Buffered(1))


def _row_tile(seq, want):
    t = min(want, seq)
    assert seq % t == 0
    return t


def _mod_kernel(c_ref, w_ref, b_ref, o_ref):
    a = c_ref[...]
    a = a * jax.nn.sigmoid(a)
    o_ref[0] = jnp.dot(a, w_ref[0], preferred_element_type=F32,
                       precision=lax.Precision.HIGHEST) + b_ref[0]


def _modulation(cond, w_mod, b_mod):
    n_layers = w_mod.shape[0]
    tn = D_MODEL
    return pl.pallas_call(
        _mod_kernel,
        grid=(n_layers, 6 * D_MODEL // tn),
        in_specs=[
            pl.BlockSpec((MOD_ROWS, D_MODEL), lambda l, j: (0, 0)),
            pl.BlockSpec((1, D_MODEL, tn), lambda l, j: (l, 0, j)),
            pl.BlockSpec((1, 1, tn), lambda l, j: (l, 0, j)),
        ],
        out_specs=pl.BlockSpec((1, MOD_ROWS, tn), lambda l, j: (l, 0, j)),
        out_shape=jax.ShapeDtypeStruct((n_layers, MOD_ROWS, 6 * D_MODEL), F32),
        compiler_params=_params("arbitrary", "arbitrary"),
    )(cond, w_mod, b_mod[:, None, :])


def _modulated_rmsnorm(xt, g, shift, scale):
    ms = jnp.mean(xt * xt, axis=-1, keepdims=True)
    h = xt * lax.rsqrt(ms + NORM_EPS) * g
    return h * (1.0 + scale) + shift


def _head_norm_rope(ph, gain, cos, sin, rope):
    ms = jnp.mean(ph * ph, axis=-1, keepdims=True)
    y = ph * lax.rsqrt(ms + NORM_EPS) * gain
    if rope:
        lane = lax.broadcasted_iota(jnp.int32, y.shape, 1)
        first_half = (lane % (2 * ROPE_FREQS)) < ROPE_FREQS
        partner = jnp.where(first_half,
                            pltpu.roll(y, HEAD_DIM - ROPE_FREQS, 1),
                            pltpu.roll(y, ROPE_FREQS, 1))
        y = y * cos + partner * sin
    return y


def _inproj_kernel(x_ref, mod_ref, g_ref, w_ref, qg_ref, kg_ref, cos_ref, sin_ref, *outs,
                   rope, kv_only):
    xt = x_ref[0]
    h = _modulated_rmsnorm(xt, g_ref[...], mod_ref[0, :, 0:D_MODEL],
                           mod_ref[0, :, D_MODEL:2 * D_MODEL])
    hb = h.astype(BF16)
    base = OFF_K if kv_only else 0

    def proj(off, n):
        return jnp.dot(hb, w_ref[:, off - base:off - base + n], preferred_element_type=F32)

    cos = cos_ref[...] if rope else None
    sin = sin_ref[...] if rope else None
    if kv_only:
        k_out, vt_out = outs
    else:
        q_out, k_out, vt_out, bg_out, cg_out, u_out, gate_out = outs

    pk = proj(OFF_K, KV_WIDTH)
    for hd in range(N_KV_HEADS):
        sl = slice(hd * HEAD_DIM, (hd + 1) * HEAD_DIM)
        k_out[0, :, sl] = _head_norm_rope(pk[:, sl], kg_ref[...], cos, sin, rope).astype(BF16)
    vt_out[0] = proj(OFF_V, KV_WIDTH).T.astype(BF16)
    if kv_only:
        return

    pq = proj(OFF_Q, ATTN_WIDTH)
    scale = HEAD_DIM ** -0.5
    for hd in range(N_Q_HEADS):
        sl = slice(hd * HEAD_DIM, (hd + 1) * HEAD_DIM)
        qh = _head_norm_rope(pq[:, sl], qg_ref[...], cos, sin, rope)
        q_out[0, :, sl] = (qh * scale).astype(BF16)

    bg_out[0] = proj(OFF_SC, D_MODEL).astype(BF16)
    cg_out[0] = (proj(OFF_SC + D_MODEL, D_MODEL) * proj(OFF_SC + 2 * D_MODEL, D_MODEL)).astype(BF16)
    cf_a = proj(OFF_CF, D_MODEL)
    u_out[0] = (cf_a * jax.nn.sigmoid(proj(OFF_CF + D_MODEL, D_MODEL))).astype(BF16)
    for j in range(3):
        sl = slice(j * D_MODEL, (j + 1) * D_MODEL)
        gate_out[0, :, sl] = jax.nn.sigmoid(proj(OFF_GATE + j * D_MODEL, D_MODEL)).astype(BF16)


def _inproj(x, mod, g_norm, w, q_gain, k_gain, cos, sin, *, rope, kv_only, tm):
    bsz, seq, _ = x.shape
    tm = _row_tile(seq, tm)
    per_batch = mod.shape[0] > 1
    tok = lambda width: pl.BlockSpec((1, tm, width), lambda b, i: (b, i, 0))
    tok_shape = lambda width: jax.ShapeDtypeStruct((bsz, seq, width), BF16)
    out_specs = [tok(KV_WIDTH), pl.BlockSpec((1, KV_WIDTH, tm), lambda b, i: (b, 0, i))]
    out_shape = [tok_shape(KV_WIDTH), jax.ShapeDtypeStruct((bsz, KV_WIDTH, seq), BF16)]
    if not kv_only:
        out_specs = [tok(ATTN_WIDTH)] + out_specs + [tok(D_MODEL)] * 3 + [tok(3 * D_MODEL)]
        out_shape = ([tok_shape(ATTN_WIDTH)] + out_shape + [tok_shape(D_MODEL)] * 3
                     + [tok_shape(3 * D_MODEL)])
    return pl.pallas_call(
        functools.partial(_inproj_kernel, rope=rope, kv_only=kv_only),
        grid=(bsz, seq // tm),
        in_specs=[
            pl.BlockSpec((1, tm, D_MODEL), lambda b, i: (b, i, 0)),
            pl.BlockSpec((1, 1, 6 * D_MODEL), (lambda b, i: (b, 0, 0)) if per_batch
                         else (lambda b, i: (0, 0, 0))),
            _resident((1, D_MODEL)),
            _resident(w.shape),
            _resident((1, HEAD_DIM)),
            _resident((1, HEAD_DIM)),
            pl.BlockSpec((tm, HEAD_DIM), lambda b, i: (i, 0)),
            pl.BlockSpec((tm, HEAD_DIM), lambda b, i: (i, 0)),
        ],
        out_specs=out_specs,
        out_shape=out_shape,
        compiler_params=_params("parallel", "parallel"),
    )(x, mod, g_norm, w, q_gain, k_gain, cos, sin)


def _attn_kernel(q_ref, k_ref, vt_ref, o_ref, *, kc):
    n_keys = k_ref.shape[1]
    tq = q_ref.shape[1]
    n_chunks = n_keys // kc
    for hh in range(Q_PER_KV):
        sl = slice(hh * HEAD_DIM, (hh + 1) * HEAD_DIM)
        qh = q_ref[0, :, sl]

        def chunk(c, carry, qh=qh):
            m, l, acc = carry
            k0 = pl.multiple_of(c * kc, kc)
            kch = k_ref[0, pl.ds(k0, kc), :]
            st = lax.dot_general(kch, qh, (((1,), (1,)), ((), ())),
                                 preferred_element_type=F32)
            m_new = jnp.maximum(m, jnp.max(st, axis=0, keepdims=True))
            alpha = jnp.exp(m - m_new)
            p = jnp.exp(st - m_new)
            l = alpha * l + jnp.sum(p, axis=0, keepdims=True)
            vch = vt_ref[0, :, pl.ds(k0, kc)]
            acc = acc * alpha + jnp.dot(vch, p.astype(BF16), preferred_element_type=F32)
            return m_new, l, acc

        init = (jnp.full((1, tq), -jnp.inf, F32), jnp.zeros((1, tq), F32),
                jnp.zeros((HEAD_DIM, tq), F32))
        _, l, acc = lax.fori_loop(0, n_chunks, chunk, init)
        o_ref[0, :, sl] = (acc / l).T.astype(BF16)


def _key_chunk(n_keys):
    for kc in (768, 1024, 512, 640, 256, 128):
        if n_keys % kc == 0:
            return kc
    raise ValueError(f"unsupported key count {n_keys}")


def _attention(q, k, vt, *, tq):
    bsz, seq, _ = q.shape
    n_keys = k.shape[1]
    tq = _row_tile(seq, tq)
    gw = Q_PER_KV * HEAD_DIM
    return pl.pallas_call(
        functools.partial(_attn_kernel, kc=_key_chunk(n_keys)),
        grid=(bsz, N_KV_HEADS, seq // tq),
        in_specs=[
            pl.BlockSpec((1, tq, gw), lambda b, g, i: (b, i, g)),
            pl.BlockSpec((1, n_keys, HEAD_DIM), lambda b, g, i: (b, 0, g)),
            pl.BlockSpec((1, HEAD_DIM, n_keys), lambda b, g, i: (b, g, 0)),
        ],
        out_specs=pl.BlockSpec((1, tq, gw), lambda b, g, i: (b, i, g)),
        out_shape=jax.ShapeDtypeStruct((bsz, seq, ATTN_WIDTH), BF16),
        compiler_params=_params("parallel", "parallel", "arbitrary"),
    )(q, k, vt)


def _fill_ext(ext, prev_ref, main_ref, next_ref, tm):
    i = pl.program_id(1)
    first = i == 0
    last = i == pl.num_programs(1) - 1
    ext[0:HALO, :] = jnp.where(first, 0.0, prev_ref[0].astype(F32))
    ext[HALO:HALO + tm, :] = main_ref[0].astype(F32)
    ext[HALO + tm:2 * HALO + tm, :] = jnp.where(last, 0.0, next_ref[0].astype(F32))


def _dwconv(ext, shifted, w_ref, taps, out, tm):
    offs = [HALO + k - taps // 2 for k in range(taps)]
    n = tm + 2 * HALO - SUBLANES
    for r in sorted({o % SUBLANES for o in offs} - {0}):
        shifted[r, 0:n, :] = ext[r:r + n, :]

    def body(c, _):
        r0 = pl.multiple_of(c * CONV_ROWS, CONV_ROWS)
        acc = jnp.zeros((CONV_ROWS, D_MODEL), F32)
        for k, o in enumerate(offs):
            r = o % SUBLANES
            src = ext if r == 0 else shifted.at[r]
            acc = acc + src[pl.ds(r0 + (o - r), CONV_ROWS), :] * w_ref[k:k + 1, :]
        out[pl.ds(r0, CONV_ROWS), :] = acc
        return 0
    lax.fori_loop(0, tm // CONV_ROWS, body, 0)


def _mix_kernel(x_ref, attn_ref, bg_ref, cg_ref, cgp_ref, cgn_ref, u_ref, up_ref, un_ref,
                gate_ref, mod_ref, wao_ref, wsc_ref, wcf_ref, wo_ref, wscc_ref, wcfc_ref,
                bcfc_ref, gln_ref, bln_ref, bcfo_ref, o_ref, ext, shifted, conv):
    tm = x_ref.shape[1]
    dot = functools.partial(jnp.dot, preferred_element_type=F32)

    y_attn = dot(attn_ref[0], wao_ref[...])

    _fill_ext(ext, cgp_ref, cg_ref, cgn_ref, tm)
    _dwconv(ext, shifted, wscc_ref, SC_KERNEL, conv, tm)
    y_sc = dot((bg_ref[0].astype(F32) * conv[...]).astype(BF16), wsc_ref[...])

    _fill_ext(ext, up_ref, u_ref, un_ref, tm)
    _dwconv(ext, shifted, wcfc_ref, CF_KERNEL, conv, tm)
    uc = conv[...] + bcfc_ref[...]
    mu = jnp.mean(uc, axis=-1, keepdims=True)
    xc = uc - mu
    un = xc * lax.rsqrt(jnp.mean(xc * xc, axis=-1, keepdims=True) + LN_EPS)
    un = un * gln_ref[...] + bln_ref[...]
    un = un * jax.nn.sigmoid(un)
    y_cf = dot(un.astype(BF16), wcf_ref[...]) + bcfo_ref[...]

    ga = gate_ref[0, :, 0:D_MODEL].astype(F32)
    gb = gate_ref[0, :, D_MODEL:2 * D_MODEL].astype(F32)
    gc = gate_ref[0, :, 2 * D_MODEL:3 * D_MODEL].astype(F32)
    merged = ga * y_attn + gb * y_sc + gc * y_cf
    y = dot(merged.astype(BF16), wo_ref[...])
    gt1 = mod_ref[0, :, 2 * D_MODEL:3 * D_MODEL]
    o_ref[0] = x_ref[0] + gt1 * y


def _mixer(x, attn, bg, cg, u, gates, mod, lw, *, tm):
    bsz, seq, _ = x.shape
    tm = _row_tile(seq, tm)
    per_batch = mod.shape[0] > 1
    hb = tm // HALO
    n_halo = seq // HALO
    tok = lambda width: pl.BlockSpec((1, tm, width), lambda b, i: (b, i, 0))
    prev = pl.BlockSpec((1, HALO, D_MODEL), lambda b, i: (b, jnp.maximum(i * hb - 1, 0), 0))
    nxt = pl.BlockSpec((1, HALO, D_MODEL),
                       lambda b, i: (b, jnp.minimum((i + 1) * hb, n_halo - 1), 0))
    row = _resident((1, D_MODEL))
    sq = _resident((D_MODEL, D_MODEL))
    return pl.pallas_call(
        _mix_kernel,
        grid=(bsz, seq // tm),
        in_specs=[
            tok(D_MODEL), tok(ATTN_WIDTH), tok(D_MODEL),
            tok(D_MODEL), prev, nxt,
            tok(D_MODEL), prev, nxt,
            tok(3 * D_MODEL),
            pl.BlockSpec((1, 1, 6 * D_MODEL), (lambda b, i: (b, 0, 0)) if per_batch
                         else (lambda b, i: (0, 0, 0))),
            sq, sq, sq, sq,
            _resident((SC_KERNEL, D_MODEL)), _resident((CF_KERNEL, D_MODEL)),
            row, row, row, row,
        ],
        out_specs=tok(D_MODEL),
        out_shape=jax.ShapeDtypeStruct((bsz, seq, D_MODEL), F32),
        scratch_shapes=[pltpu.VMEM((tm + 2 * HALO, D_MODEL), F32),
                        pltpu.VMEM((SUBLANES, tm + 2 * HALO, D_MODEL), F32),
                        pltpu.VMEM((tm, D_MODEL), F32)],
        compiler_params=_params("parallel", "arbitrary"),
    )(x, attn, bg, cg, cg, cg, u, u, u, gates, mod,
      lw["w_attn_out"], lw["w_sc_out"], lw["w_cf_out"], lw["w_o"],
      lw["w_sc_conv"], lw["w_cf_conv"], lw["b_cf_conv"], lw["g_cf_ln"], lw["b_cf_ln"],
      lw["b_cf_out"])


def _mlp_kernel(x_ref, mod_ref, g_ref, w1_ref, w2_ref, gf_ref, o_ref, *, final_norm):
    xt = x_ref[0]
    h = _modulated_rmsnorm(xt, g_ref[...], mod_ref[0, :, 3 * D_MODEL:4 * D_MODEL],
                           mod_ref[0, :, 4 * D_MODEL:5 * D_MODEL])
    hb = h.astype(BF16)
    acc = jnp.zeros(xt.shape, F32)
    for j in range(D_FF // D_MODEL):
        sl = slice(j * D_MODEL, (j + 1) * D_MODEL)
        hid = jnp.dot(hb, w1_ref[:, sl], preferred_element_type=F32)
        hid = jnp.square(jnp.maximum(hid, 0.0)).astype(BF16)
        acc = acc + jnp.dot(hid, w2_ref[sl, :], preferred_element_type=F32)
    y = xt + mod_ref[0, :, 5 * D_MODEL:6 * D_MODEL] * acc
    if final_norm:
        ms = jnp.mean(y * y, axis=-1, keepdims=True)
        y = y * lax.rsqrt(ms + NORM_EPS) * gf_ref[...]
    o_ref[0] = y


def _mlp(x, mod, g_norm, w1, w2, g_final, *, final_norm, tm):
    bsz, seq, _ = x.shape
    tm = _row_tile(seq, tm)
    per_batch = mod.shape[0] > 1
    tok = pl.BlockSpec((1, tm, D_MODEL), lambda b, i: (b, i, 0))
    return pl.pallas_call(
        functools.partial(_mlp_kernel, final_norm=final_norm),
        grid=(bsz, seq // tm),
        in_specs=[
            tok,
            pl.BlockSpec((1, 1, 6 * D_MODEL), (lambda b, i: (b, 0, 0)) if per_batch
                         else (lambda b, i: (0, 0, 0))),
            _resident((1, D_MODEL)),
            _resident((D_MODEL, D_FF)),
            _resident((D_FF, D_MODEL)),
            _resident((1, D_MODEL)),
        ],
        out_specs=tok,
        out_shape=jax.ShapeDtypeStruct((bsz, seq, D_MODEL), F32),
        compiler_params=_params("parallel", "parallel"),
    )(x, mod, g_norm, w1, w2, g_final)


def _rope_tables(n_tokens):
    rows = n_tokens // GRID_W
    row = jnp.repeat(jnp.arange(rows), GRID_W)
    col = jnp.tile(jnp.arange(GRID_W), rows)
    pos = jnp.stack([row, col], axis=-1).astype(F32)
    inv_freq = ROPE_THETA ** (-jnp.arange(ROPE_FREQS, dtype=F32) * 2.0 / ROPE_AXIS_DIM)
    ang = pos[:, :, None] * inv_freq
    cos, sin = jnp.cos(ang), jnp.sin(ang)
    cos_t = jnp.concatenate([cos, cos], axis=-1).reshape(n_tokens, HEAD_DIM)
    sin_t = jnp.concatenate([-sin, sin], axis=-1).reshape(n_tokens, HEAD_DIM)
    return cos_t, sin_t


def kernel(x, c, ctx, c_ctx, w_mod, b_mod, g_norm1, g_norm2, w_in, q_gain, k_gain, w_attn_out,
           w_sc_conv, w_sc_out, w_cf_conv, b_cf_conv, g_cf_ln, b_cf_ln, w_cf_out, b_cf_out,
           w_o, w_mlp_in, w_mlp_out, g_final):
    bsz, seq, _ = x.shape
    n_ctx = ctx.shape[1]
    depth = w_mod.shape[0]
    assert bsz + 1 <= MOD_ROWS and seq % GRID_W == 0
    row = lambda v: v.reshape(1, -1)

    cond = jnp.zeros((MOD_ROWS, D_MODEL), F32).at[:bsz].set(c).at[bsz].set(c_ctx)
    mod_all = _modulation(cond, w_mod, b_mod)
    cos_lat, sin_lat = _rope_tables(seq)
    cos_ctx, sin_ctx = cos_lat[:n_ctx], sin_lat[:n_ctx]

    for l in range(depth):
        last = l == depth - 1
        mod_lat = mod_all[l, :bsz][:, None, :]
        mod_ctx = mod_all[l, bsz:bsz + 1][:, None, :]
        w_in_l = w_in[l].astype(BF16)
        lw = dict(
            w_attn_out=w_attn_out[l].astype(BF16), w_sc_out=w_sc_out[l].astype(BF16),
            w_cf_out=w_cf_out[l].astype(BF16), w_o=w_o[l].astype(BF16),
            w_sc_conv=w_sc_conv[l], w_cf_conv=w_cf_conv[l], b_cf_conv=row(b_cf_conv[l]),
            g_cf_ln=row(g_cf_ln[l]), b_cf_ln=row(b_cf_ln[l]), b_cf_out=row(b_cf_out[l]))
        w1 = w_mlp_in[l].astype(BF16)
        w2 = w_mlp_out[l].astype(BF16)
        gains = (row(q_gain[l]), row(k_gain[l]))

        if last:
            k_ctx, vt_ctx = _inproj(ctx, mod_ctx, row(g_norm1[l]), w_in_l[:, OFF_K:OFF_SC], *gains,
                                    cos_ctx, sin_ctx, rope=False, kv_only=True, tm=256)
        else:
            q_c, k_ctx, vt_ctx, bg_c, cg_c, u_c, gates_c = _inproj(
                ctx, mod_ctx, row(g_norm1[l]), w_in_l, *gains, cos_ctx, sin_ctx,
                rope=False, kv_only=False, tm=256)

        q, k_lat, vt_lat, bg, cg, u, gates = _inproj(
            x, mod_lat, row(g_norm1[l]), w_in_l, *gains, cos_lat, sin_lat,
            rope=True, kv_only=False, tm=512)
        k_all = jnp.concatenate([k_ctx, k_lat], axis=1)
        vt_all = jnp.concatenate([vt_ctx, vt_lat], axis=2)
        attn = _attention(q, k_all, vt_all, tq=256)
        x = _mixer(x, attn, bg, cg, u, gates, mod_lat, lw, tm=256)
        x = _mlp(x, mod_lat, row(g_norm2[l]), w1, w2, row(g_final), final_norm=last, tm=512)

        if not last:
            attn_c = _attention(q_c, k_ctx, vt_ctx, tq=256)
            ctx = _mixer(ctx, attn_c, bg_c, cg_c, u_c, gates_c, mod_ctx, lw, tm=256)
            ctx = _mlp(ctx, mod_ctx, row(g_norm2[l]), w1, w2, row(g_final), final_norm=False, tm=256)
    return x
```

```python
import functools

import jax
import jax.numpy as jnp
from jax import lax
from jax.experimental import pallas as pl
from jax.experimental.pallas import tpu as pltpu

D_MODEL = 1024
GRID_W = 64
HEAD_DIM = 128
N_Q_HEADS = 8
N_KV_HEADS = 2
Q_PER_KV = N_Q_HEADS // N_KV_HEADS
ATTN_WIDTH = N_Q_HEADS * HEAD_DIM
KV_WIDTH = N_KV_HEADS * HEAD_DIM
ROPE_THETA = 10000.0
ROPE_AXIS_DIM = HEAD_DIM // 2
ROPE_FREQS = ROPE_AXIS_DIM // 2
SC_KERNEL = 3
CF_KERNEL = 31
D_FF = 4 * D_MODEL
NORM_EPS = 1e-6
LN_EPS = 1e-5
LOG2E = 1.4426950408889634
Q_SCALE_LOG2 = HEAD_DIM ** -0.5 * LOG2E
SAFE_SCORE_BOUND = 60.0
OFF_Q = 0
OFF_K = OFF_Q + ATTN_WIDTH
OFF_V = OFF_K + KV_WIDTH
OFF_SC = OFF_V + KV_WIDTH
OFF_CF = OFF_SC + 3 * D_MODEL
OFF_GATE = OFF_CF + 2 * D_MODEL
D_IN = OFF_GATE + 3 * D_MODEL

V7X_VMEM_BYTES = 64 * 1024 * 1024
VMEM_LIMIT_BYTES = V7X_VMEM_BYTES - 8 * 1024 * 1024
SUBLANES = 8
HALO = 16
MOD_ROWS = 16
CONV_ROWS = 16

F32 = jnp.float32
BF16 = jnp.bfloat16


def _params(*sem):
    return pltpu.CompilerParams(dimension_semantics=sem, vmem_limit_bytes=VMEM_LIMIT_BYTES)


def _resident(shape):
    nd = len(shape)
    return pl.BlockSpec(shape, lambda *_: (0,) * nd, pipeline_mode=pl.Buffered(1))


def _row_tile(seq, want):
    t = min(want, seq)
    assert seq % t == 0
    return t


def _mod_kernel(c_ref, w_ref, b_ref, o_ref):
    a = c_ref[...]
    a = a * jax.nn.sigmoid(a)
    o_ref[0] = jnp.dot(a, w_ref[0], preferred_element_type=F32,
                       precision=lax.Precision.HIGHEST) + b_ref[0]


def _modulation(cond, w_mod, b_mod):
    n_layers = w_mod.shape[0]
    tn = D_MODEL
    return pl.pallas_call(
        _mod_kernel,
        grid=(n_layers, 6 * D_MODEL // tn),
        in_specs=[
            pl.BlockSpec((MOD_ROWS, D_MODEL), lambda l, j: (0, 0)),
            pl.BlockSpec((1, D_MODEL, tn), lambda l, j: (l, 0, j)),
            pl.BlockSpec((1, 1, tn), lambda l, j: (l, 0, j)),
        ],
        out_specs=pl.BlockSpec((1, MOD_ROWS, tn), lambda l, j: (l, 0, j)),
        out_shape=jax.ShapeDtypeStruct((n_layers, MOD_ROWS, 6 * D_MODEL), F32),
        compiler_params=_params("arbitrary", "arbitrary"),
    )(cond, w_mod, b_mod[:, None, :])


def _modulated_rmsnorm(xt, g, shift, scale):
    ms = jnp.mean(xt * xt, axis=-1, keepdims=True)
    h = xt * lax.rsqrt(ms + NORM_EPS) * g
    return h * (1.0 + scale) + shift


def _head_norm_rope(ph, gain, cos, sin, rope):
    ms = jnp.mean(ph * ph, axis=-1, keepdims=True)
    y = ph * lax.rsqrt(ms + NORM_EPS) * gain
    if rope:
        lane = lax.broadcasted_iota(jnp.int32, y.shape, 1)
        first_half = (lane % (2 * ROPE_FREQS)) < ROPE_FREQS
        partner = jnp.where(first_half,
                            pltpu.roll(y, HEAD_DIM - ROPE_FREQS, 1),
                            pltpu.roll(y, ROPE_FREQS, 1))
        y = y * cos + partner * sin
    return y


def _inproj_kernel(x_ref, mod_ref, g_ref, w_ref, qg_ref, kg_ref, cos_ref, sin_ref, *outs,
                   rope, kv_only):
    xt = x_ref[0]
    h = _modulated_rmsnorm(xt, g_ref[...], mod_ref[0, :, 0:D_MODEL],
                           mod_ref[0, :, D_MODEL:2 * D_MODEL])
    hb = h.astype(BF16)
    base = OFF_K if kv_only else 0

    def proj(off, n):
        return jnp.dot(hb, w_ref[:, off - base:off - base + n], preferred_element_type=F32)

    cos = cos_ref[...] if rope else None
    sin = sin_ref[...] if rope else None
    if kv_only:
        k_out, vt_out = outs
    else:
        q_out, k_out, vt_out, bg_out, cg_out, u_out, gate_out = outs

    pk = proj(OFF_K, KV_WIDTH)
    for hd in range(N_KV_HEADS):
        sl = slice(hd * HEAD_DIM, (hd + 1) * HEAD_DIM)
        k_out[0, :, sl] = _head_norm_rope(pk[:, sl], kg_ref[...], cos, sin, rope).astype(BF16)
    vt_out[0] = proj(OFF_V, KV_WIDTH).T.astype(BF16)
    if kv_only:
        return

    pq = proj(OFF_Q, ATTN_WIDTH)
    for hd in range(N_Q_HEADS):
        sl = slice(hd * HEAD_DIM, (hd + 1) * HEAD_DIM)
        qh = _head_norm_rope(pq[:, sl], qg_ref[...], cos, sin, rope)
        q_out[0, hd] = (qh * Q_SCALE_LOG2).astype(BF16)

    bg_out[0] = proj(OFF_SC, D_MODEL).astype(BF16)
    cg_out[0] = (proj(OFF_SC + D_MODEL, D_MODEL) * proj(OFF_SC + 2 * D_MODEL, D_MODEL)).astype(BF16)
    cf_a = proj(OFF_CF, D_MODEL)
    u_out[0] = (cf_a * jax.nn.sigmoid(proj(OFF_CF + D_MODEL, D_MODEL))).astype(BF16)
    for j in range(3):
        sl = slice(j * D_MODEL, (j + 1) * D_MODEL)
        gate_out[0, :, sl] = jax.nn.sigmoid(proj(OFF_GATE + j * D_MODEL, D_MODEL)).astype(BF16)


def _inproj(x, mod, g_norm, w, q_gain, k_gain, cos, sin, *, rope, kv_only, tm):
    bsz, seq, _ = x.shape
    tm = _row_tile(seq, tm)
    per_batch = mod.shape[0] > 1
    tok = lambda width: pl.BlockSpec((1, tm, width), lambda b, i: (b, i, 0))
    tok_shape = lambda width: jax.ShapeDtypeStruct((bsz, seq, width), BF16)
    out_specs = [tok(KV_WIDTH), pl.BlockSpec((1, KV_WIDTH, tm), lambda b, i: (b, 0, i))]
    out_shape = [tok_shape(KV_WIDTH), jax.ShapeDtypeStruct((bsz, KV_WIDTH, seq), BF16)]
    if not kv_only:
        q_spec = pl.BlockSpec((1, N_Q_HEADS, tm, HEAD_DIM), lambda b, i: (b, 0, i, 0))
        q_shape = jax.ShapeDtypeStruct((bsz, N_Q_HEADS, seq, HEAD_DIM), BF16)
        out_specs = [q_spec] + out_specs + [tok(D_MODEL)] * 3 + [tok(3 * D_MODEL)]
        out_shape = [q_shape] + out_shape + [tok_shape(D_MODEL)] * 3 + [tok_shape(3 * D_MODEL)]
    return pl.pallas_call(
        functools.partial(_inproj_kernel, rope=rope, kv_only=kv_only),
        grid=(bsz, seq // tm),
        in_specs=[
            pl.BlockSpec((1, tm, D_MODEL), lambda b, i: (b, i, 0)),
            pl.BlockSpec((1, 1, 6 * D_MODEL), (lambda b, i: (b, 0, 0)) if per_batch
                         else (lambda b, i: (0, 0, 0))),
            _resident((1, D_MODEL)),
            _resident(w.shape),
            _resident((1, HEAD_DIM)),
            _resident((1, HEAD_DIM)),
            pl.BlockSpec((tm, HEAD_DIM), lambda b, i: (i, 0)),
            pl.BlockSpec((tm, HEAD_DIM), lambda b, i: (i, 0)),
        ],
        out_specs=out_specs,
        out_shape=out_shape,
        compiler_params=_params("parallel", "parallel"),
    )(x, mod, g_norm, w, q_gain, k_gain, cos, sin)


def _scores_t(k_rows, qh):
    return lax.dot_general(k_rows, qh, (((1,), (1,)), ((), ())), preferred_element_type=F32)


def _attn_bounded_kernel(q_ref, k_ref, vt_ref, o_ref, *, kc):
    n_keys = k_ref.shape[1]
    tq = q_ref.shape[2]

    def head(h, _):
        qh = q_ref[0, h]
        acc = jnp.zeros((HEAD_DIM, tq), F32)
        lsum = jnp.zeros((SUBLANES, tq), F32)
        for k0 in range(0, n_keys, kc):
            p = jnp.exp2(_scores_t(k_ref[0, k0:k0 + kc, :], qh))
            lsum = lsum + jnp.sum(p.reshape(kc // SUBLANES, SUBLANES, tq), axis=0)
            acc = acc + jnp.dot(vt_ref[0, :, k0:k0 + kc], p.astype(BF16),
                                preferred_element_type=F32)
        inv_l = 1.0 / jnp.sum(lsum, axis=0, keepdims=True)
        o_ref[0, h] = (acc * inv_l).T.astype(BF16)
        return 0

    lax.fori_loop(0, Q_PER_KV, head, 0)


def _attn_online_kernel(q_ref, k_ref, vt_ref, o_ref, *, kc):
    n_keys = k_ref.shape[1]
    tq = q_ref.shape[2]

    def head(h, _):
        qh = q_ref[0, h]

        def chunk(c, carry):
            m, l, acc = carry
            k0 = pl.multiple_of(c * kc, kc)
            st = _scores_t(k_ref[0, pl.ds(k0, kc), :], qh)
            m_new = jnp.maximum(m, jnp.max(st, axis=0, keepdims=True))
            alpha = jnp.exp2(m - m_new)
            p = jnp.exp2(st - m_new)
            l = alpha * l + jnp.sum(p, axis=0, keepdims=True)
            acc = acc * alpha + jnp.dot(vt_ref[0, :, pl.ds(k0, kc)], p.astype(BF16),
                                        preferred_element_type=F32)
            return m_new, l, acc

        init = (jnp.full((1, tq), -jnp.inf, F32), jnp.zeros((1, tq), F32),
                jnp.zeros((HEAD_DIM, tq), F32))
        _, l, acc = lax.fori_loop(0, n_keys // kc, chunk, init)
        o_ref[0, h] = (acc / l).T.astype(BF16)
        return 0

    lax.fori_loop(0, Q_PER_KV, head, 0)


def _key_chunk(n_keys):
    for kc in (768, 1024, 512, 640, 256, 128):
        if n_keys % kc == 0:
            return kc
    raise ValueError(f"unsupported key count {n_keys}")


def _attention(q, k, vt, *, tq, bounded):
    bsz, _, seq, _ = q.shape
    n_keys = k.shape[1]
    tq = _row_tile(seq, tq)
    body = _attn_bounded_kernel if bounded else _attn_online_kernel
    qo_spec = pl.BlockSpec((1, Q_PER_KV, tq, HEAD_DIM), lambda b, g, i: (b, g, i, 0))
    return pl.pallas_call(
        functools.partial(body, kc=_key_chunk(n_keys)),
        grid=(bsz, N_KV_HEADS, seq // tq),
        in_specs=[
            qo_spec,
            pl.BlockSpec((1, n_keys, HEAD_DIM), lambda b, g, i: (b, 0, g)),
            pl.BlockSpec((1, HEAD_DIM, n_keys), lambda b, g, i: (b, g, 0)),
        ],
        out_specs=qo_spec,
        out_shape=jax.ShapeDtypeStruct((bsz, N_Q_HEADS, seq, HEAD_DIM), BF16),
        compiler_params=_params("parallel", "parallel", "arbitrary"),
    )(q, k, vt)


def _attention_any(q, k, vt, scores_bounded, *, tq):
    return lax.cond(scores_bounded,
                    functools.partial(_attention, tq=tq, bounded=True),
                    functools.partial(_attention, tq=tq, bounded=False),
                    q, k, vt)


def _fill_ext(ext, prev_ref, main_ref, next_ref, tm):
    i = pl.program_id(1)
    first = i == 0
    last = i == pl.num_programs(1) - 1
    ext[0:HALO, :] = jnp.where(first, 0.0, prev_ref[0].astype(F32))
    ext[HALO:HALO + tm, :] = main_ref[0].astype(F32)
    ext[HALO + tm:2 * HALO + tm, :] = jnp.where(last, 0.0, next_ref[0].astype(F32))


def _dwconv(ext, shifted, w_ref, taps, out, tm):
    offs = [HALO + k - taps // 2 for k in range(taps)]
    n = tm + 2 * HALO - SUBLANES
    for r in sorted({o % SUBLANES for o in offs} - {0}):
        shifted[r, 0:n, :] = ext[r:r + n, :]

    def body(c, _):
        r0 = pl.multiple_of(c * CONV_ROWS, CONV_ROWS)
        acc = jnp.zeros((CONV_ROWS, D_MODEL), F32)
        for k, o in enumerate(offs):
            r = o % SUBLANES
            src = ext if r == 0 else shifted.at[r]
            acc = acc + src[pl.ds(r0 + (o - r), CONV_ROWS), :] * w_ref[k:k + 1, :]
        out[pl.ds(r0, CONV_ROWS), :] = acc
        return 0
    lax.fori_loop(0, tm // CONV_ROWS, body, 0)


def _mix_kernel(x_ref, attn_ref, bg_ref, cg_ref, cgp_ref, cgn_ref, u_ref, up_ref, un_ref,
                gate_ref, mod_ref, wao_ref, wsc_ref, wcf_ref, wo_ref, wscc_ref, wcfc_ref,
                bcfc_ref, gln_ref, bln_ref, bcfo_ref, o_ref, ext, shifted, conv):
    tm = x_ref.shape[1]
    dot = functools.partial(jnp.dot, preferred_element_type=F32)

    attn = jnp.concatenate([attn_ref[0, hd] for hd in range(N_Q_HEADS)], axis=-1)
    y_attn = dot(attn, wao_ref[...])

    _fill_ext(ext, cgp_ref, cg_ref, cgn_ref, tm)
    _dwconv(ext, shifted, wscc_ref, SC_KERNEL, conv, tm)
    y_sc = dot((bg_ref[0].astype(F32) * conv[...]).astype(BF16), wsc_ref[...])

    _fill_ext(ext, up_ref, u_ref, un_ref, tm)
    _dwconv(ext, shifted, wcfc_ref, CF_KERNEL, conv, tm)
    uc = conv[...] + bcfc_ref[...]
    mu = jnp.mean(uc, axis=-1, keepdims=True)
    xc = uc - mu
    un = xc * lax.rsqrt(jnp.mean(xc * xc, axis=-1, keepdims=True) + LN_EPS)
    un = un * gln_ref[...] + bln_ref[...]
    un = un * jax.nn.sigmoid(un)
    y_cf = dot(un.astype(BF16), wcf_ref[...]) + bcfo_ref[...]

    ga = gate_ref[0, :, 0:D_MODEL].astype(F32)
    gb = gate_ref[0, :, D_MODEL:2 * D_MODEL].astype(F32)
    gc = gate_ref[0, :, 2 * D_MODEL:3 * D_MODEL].astype(F32)
    merged = ga * y_attn + gb * y_sc + gc * y_cf
    y = dot(merged.astype(BF16), wo_ref[...])
    gt1 = mod_ref[0, :, 2 * D_MODEL:3 * D_MODEL]
    o_ref[0] = x_ref[0] + gt1 * y


def _mixer(x, attn, bg, cg, u, gates, mod, lw, *, tm):
    bsz, seq, _ = x.shape
    tm = _row_tile(seq, tm)
    per_batch = mod.shape[0] > 1
    hb = tm // HALO
    n_halo = seq // HALO
    tok = lambda width: pl.BlockSpec((1, tm, width), lambda b, i: (b, i, 0))
    prev = pl.BlockSpec((1, HALO, D_MODEL), lambda b, i: (b, jnp.maximum(i * hb - 1, 0), 0))
    nxt = pl.BlockSpec((1, HALO, D_MODEL),
                       lambda b, i: (b, jnp.minimum((i + 1) * hb, n_halo - 1), 0))
    row = _resident((1, D_MODEL))
    sq = _resident((D_MODEL, D_MODEL))
    return pl.pallas_call(
        _mix_kernel,
        grid=(bsz, seq // tm),
        in_specs=[
            tok(D_MODEL),
            pl.BlockSpec((1, N_Q_HEADS, tm, HEAD_DIM), lambda b, i: (b, 0, i, 0)),
            tok(D_MODEL),
            tok(D_MODEL), prev, nxt,
            tok(D_MODEL), prev, nxt,
            tok(3 * D_MODEL),
            pl.BlockSpec((1, 1, 6 * D_MODEL), (lambda b, i: (b, 0, 0)) if per_batch
                         else (lambda b, i: (0, 0, 0))),
            sq, sq, sq, sq,
            _resident((SC_KERNEL, D_MODEL)), _resident((CF_KERNEL, D_MODEL)),
            row, row, row, row,
        ],
        out_specs=tok(D_MODEL),
        out_shape=jax.ShapeDtypeStruct((bsz, seq, D_MODEL), F32),
        scratch_shapes=[pltpu.VMEM((tm + 2 * HALO, D_MODEL), F32),
                        pltpu.VMEM((SUBLANES, tm + 2 * HALO, D_MODEL), F32),
                        pltpu.VMEM((tm, D_MODEL), F32)],
        compiler_params=_params("parallel", "arbitrary"),
    )(x, attn, bg, cg, cg, cg, u, u, u, gates, mod,
      lw["w_attn_out"], lw["w_sc_out"], lw["w_cf_out"], lw["w_o"],
      lw["w_sc_conv"], lw["w_cf_conv"], lw["b_cf_conv"], lw["g_cf_ln"], lw["b_cf_ln"],
      lw["b_cf_out"])


def _mlp_kernel(x_ref, mod_ref, g_ref, w1_ref, w2_ref, gf_ref, o_ref, *, final_norm):
    xt = x_ref[0]
    h = _modulated_rmsnorm(xt, g_ref[...], mod_ref[0, :, 3 * D_MODEL:4 * D_MODEL],
                           mod_ref[0, :, 4 * D_MODEL:5 * D_MODEL])
    hb = h.astype(BF16)
    acc = jnp.zeros(xt.shape, F32)
    for j in range(D_FF // D_MODEL):
        sl = slice(j * D_MODEL, (j + 1) * D_MODEL)
        hid = jnp.dot(hb, w1_ref[:, sl], preferred_element_type=F32)
        hid = jnp.square(jnp.maximum(hid, 0.0)).astype(BF16)
        acc = acc + jnp.dot(hid, w2_ref[sl, :], preferred_element_type=F32)
    y = xt + mod_ref[0, :, 5 * D_MODEL:6 * D_MODEL] * acc
    if final_norm:
        ms = jnp.mean(y * y, axis=-1, keepdims=True)
        y = y * lax.rsqrt(ms + NORM_EPS) * gf_ref[...]
    o_ref[0] = y


def _mlp(x, mod, g_norm, w1, w2, g_final, *, final_norm, tm):
    bsz, seq, _ = x.shape
    tm = _row_tile(seq, tm)
    per_batch = mod.shape[0] > 1
    tok = pl.BlockSpec((1, tm, D_MODEL), lambda b, i: (b, i, 0))
    return pl.pallas_call(
        functools.partial(_mlp_kernel, final_norm=final_norm),
        grid=(bsz, seq // tm),
        in_specs=[
            tok,
            pl.BlockSpec((1, 1, 6 * D_MODEL), (lambda b, i: (b, 0, 0)) if per_batch
                         else (lambda b, i: (0, 0, 0))),
            _resident((1, D_MODEL)),
            _resident((D_MODEL, D_FF)),
            _resident((D_FF, D_MODEL)),
            _resident((1, D_MODEL)),
        ],
        out_specs=tok,
        out_shape=jax.ShapeDtypeStruct((bsz, seq, D_MODEL), F32),
        compiler_params=_params("parallel", "parallel"),
    )(x, mod, g_norm, w1, w2, g_final)


def _rope_tables(n_tokens):
    rows = n_tokens // GRID_W
    row = jnp.repeat(jnp.arange(rows), GRID_W)
    col = jnp.tile(jnp.arange(GRID_W), rows)
    pos = jnp.stack([row, col], axis=-1).astype(F32)
    inv_freq = ROPE_THETA ** (-jnp.arange(ROPE_FREQS, dtype=F32) * 2.0 / ROPE_AXIS_DIM)
    ang = pos[:, :, None] * inv_freq
    cos, sin = jnp.cos(ang), jnp.sin(ang)
    cos_t = jnp.concatenate([cos, cos], axis=-1).reshape(n_tokens, HEAD_DIM)
    sin_t = jnp.concatenate([-sin, sin], axis=-1).reshape(n_tokens, HEAD_DIM)
    return cos_t, sin_t


def kernel(x, c, ctx, c_ctx, w_mod, b_mod, g_norm1, g_norm2, w_in, q_gain, k_gain, w_attn_out,
           w_sc_conv, w_sc_out, w_cf_conv, b_cf_conv, g_cf_ln, b_cf_ln, w_cf_out, b_cf_out,
           w_o, w_mlp_in, w_mlp_out, g_final):
    bsz, seq, _ = x.shape
    n_ctx = ctx.shape[1]
    depth = w_mod.shape[0]
    assert bsz + 1 <= MOD_ROWS and seq % GRID_W == 0
    row = lambda v: v.reshape(1, -1)

    cond = jnp.zeros((MOD_ROWS, D_MODEL), F32).at[:bsz].set(c).at[bsz].set(c_ctx)
    mod_all = _modulation(cond, w_mod, b_mod)
    cos_lat, sin_lat = _rope_tables(seq)
    cos_ctx, sin_ctx = cos_lat[:n_ctx], sin_lat[:n_ctx]

    for l in range(depth):
        last = l == depth - 1
        mod_lat = mod_all[l, :bsz][:, None, :]
        mod_ctx = mod_all[l, bsz:bsz + 1][:, None, :]
        w_in_l = w_in[l].astype(BF16)
        lw = dict(
            w_attn_out=w_attn_out[l].astype(BF16), w_sc_out=w_sc_out[l].astype(BF16),
            w_cf_out=w_cf_out[l].astype(BF16), w_o=w_o[l].astype(BF16),
            w_sc_conv=w_sc_conv[l], w_cf_conv=w_cf_conv[l], b_cf_conv=row(b_cf_conv[l]),
            g_cf_ln=row(g_cf_ln[l]), b_cf_ln=row(b_cf_ln[l]), b_cf_out=row(b_cf_out[l]))
        w1 = w_mlp_in[l].astype(BF16)
        w2 = w_mlp_out[l].astype(BF16)
        gains = (row(q_gain[l]), row(k_gain[l]))
        score_bound = HEAD_DIM ** 0.5 * jnp.max(jnp.abs(q_gain[l])) * jnp.max(jnp.abs(k_gain[l]))
        bounded = score_bound <= SAFE_SCORE_BOUND

        if last:
            k_ctx, vt_ctx = _inproj(ctx, mod_ctx, row(g_norm1[l]), w_in_l[:, OFF_K:OFF_SC], *gains,
                                    cos_ctx, sin_ctx, rope=False, kv_only=True, tm=256)
        else:
            q_c, k_ctx, vt_ctx, bg_c, cg_c, u_c, gates_c = _inproj(
                ctx, mod_ctx, row(g_norm1[l]), w_in_l, *gains, cos_ctx, sin_ctx,
                rope=False, kv_only=False, tm=256)

        q, k_lat, vt_lat, bg, cg, u, gates = _inproj(
            x, mod_lat, row(g_norm1[l]), w_in_l, *gains, cos_lat, sin_lat,
            rope=True, kv_only=False, tm=512)
        k_all = jnp.concatenate([k_ctx, k_lat], axis=1)
        vt_all = jnp.concatenate([vt_ctx, vt_lat], axis=2)
        attn = _attention_any(q, k_all, vt_all, bounded, tq=256)
        x = _mixer(x, attn, bg, cg, u, gates, mod_lat, lw, tm=256)
        x = _mlp(x, mod_lat, row(g_norm2[l]), w1, w2, row(g_final), final_norm=last, tm=512)

        if not last:
            attn_c = _attention_any(q_c, k_ctx, vt_ctx, bounded, tq=256)
            ctx = _mixer(ctx, attn_c, bg_c, cg_c, u_c, gates_c, mod_ctx, lw, tm=256)
            ctx = _mlp(ctx, mod_ctx, row(g_norm2[l]), w1, w2, row(g_final), final_norm=False, tm=256)
    return x
```

```python
import functools

import jax
import jax.numpy as jnp
from jax import lax
from jax.experimental import pallas as pl
from jax.experimental.pallas import tpu as pltpu

D_MODEL = 1024
GRID_W = 64
HEAD_DIM = 128
N_Q_HEADS = 8
N_KV_HEADS = 2
Q_PER_KV = N_Q_HEADS // N_KV_HEADS
ATTN_WIDTH = N_Q_HEADS * HEAD_DIM
KV_WIDTH = N_KV_HEADS * HEAD_DIM
ROPE_THETA = 10000.0
ROPE_AXIS_DIM = HEAD_DIM // 2
ROPE_FREQS = ROPE_AXIS_DIM // 2
SC_KERNEL = 3
CF_KERNEL = 31
D_FF = 4 * D_MODEL
NORM_EPS = 1e-6
LN_EPS = 1e-5
LOG2E = 1.4426950408889634
Q_SCALE_LOG2 = HEAD_DIM ** -0.5 * LOG2E
SAFE_SCORE_BOUND = 60.0
OFF_Q = 0
OFF_K = OFF_Q + ATTN_WIDTH
OFF_V = OFF_K + KV_WIDTH
OFF_SC = OFF_V + KV_WIDTH
OFF_CF = OFF_SC + 3 * D_MODEL
OFF_GATE = OFF_CF + 2 * D_MODEL
D_IN = OFF_GATE + 3 * D_MODEL

V7X_VMEM_BYTES = 64 * 1024 * 1024
VMEM_LIMIT_BYTES = V7X_VMEM_BYTES - 8 * 1024 * 1024
SUBLANES = 8
LANES = 128
HALO = 16
MOD_ROWS = 16
CONV_ROWS = 64
CONV_COLS = D_MODEL // N_KV_HEADS
F32 = jnp.float32
BF16 = jnp.bfloat16


def _params(*sem, flags=None):
    return pltpu.CompilerParams(dimension_semantics=sem, vmem_limit_bytes=VMEM_LIMIT_BYTES,
                                flags=flags)


def _resident(shape):
    nd = len(shape)
    return pl.BlockSpec(shape, lambda *_: (0,) * nd, pipeline_mode=pl.Buffered(1))


def _mod_spec(mod):
    if mod.shape[0] > 1:
        return pl.BlockSpec((1, 1, 6 * D_MODEL), lambda b, *_: (b, 0, 0))
    return pl.BlockSpec((1, 1, 6 * D_MODEL), lambda *_: (0, 0, 0))


def _row_tile(seq, want):
    t = min(want, seq)
    assert seq % t == 0
    return t


def _mod_kernel(c_ref, w_ref, b_ref, o_ref):
    a = c_ref[...]
    a = a * jax.nn.sigmoid(a)
    o_ref[0] = jnp.dot(a, w_ref[0], preferred_element_type=F32,
                       precision=lax.Precision.HIGHEST) + b_ref[0]


def _modulation(cond, w_mod, b_mod):
    n_layers = w_mod.shape[0]
    tn = D_MODEL
    return pl.pallas_call(
        _mod_kernel,
        grid=(n_layers, 6 * D_MODEL // tn),
        in_specs=[
            pl.BlockSpec((MOD_ROWS, D_MODEL), lambda l, j: (0, 0)),
            pl.BlockSpec((1, D_MODEL, tn), lambda l, j: (l, 0, j)),
            pl.BlockSpec((1, 1, tn), lambda l, j: (l, 0, j)),
        ],
        out_specs=pl.BlockSpec((1, MOD_ROWS, tn), lambda l, j: (l, 0, j)),
        out_shape=jax.ShapeDtypeStruct((n_layers, MOD_ROWS, 6 * D_MODEL), F32),
        compiler_params=_params("arbitrary", "arbitrary"),
    )(cond, w_mod, b_mod[:, None, :])


def _modulated_rmsnorm(xt, g, shift, scale):
    ms = jnp.mean(xt * xt, axis=-1, keepdims=True)
    h = xt * lax.rsqrt(ms + NORM_EPS) * g
    return h * (1.0 + scale) + shift


def _head_norm_rope(ph, gain, cos, sin, rope):
    ms = jnp.mean(ph * ph, axis=-1, keepdims=True)
    y = ph * lax.rsqrt(ms + NORM_EPS) * gain
    if rope:
        lane = lax.broadcasted_iota(jnp.int32, y.shape, 1)
        first_half = (lane % (2 * ROPE_FREQS)) < ROPE_FREQS
        partner = jnp.where(first_half,
                            pltpu.roll(y, HEAD_DIM - ROPE_FREQS, 1),
                            pltpu.roll(y, ROPE_FREQS, 1))
        y = y * cos + partner * sin
    return y


def _inproj_kernel(x_ref, mod_ref, g_ref, w_ref, qg_ref, kg_ref, cos_ref, sin_ref, *outs,
                   rope, kv_only):
    xt = x_ref[0]
    h = _modulated_rmsnorm(xt, g_ref[...], mod_ref[0, :, 0:D_MODEL],
                           mod_ref[0, :, D_MODEL:2 * D_MODEL])
    hb = h.astype(BF16)
    base = OFF_K if kv_only else 0

    def proj(off, n):
        return jnp.dot(hb, w_ref[:, off - base:off - base + n], preferred_element_type=F32)

    cos = cos_ref[...] if rope else None
    sin = sin_ref[...] if rope else None
    if kv_only:
        k_out, vt_out = outs
    else:
        q_out, k_out, vt_out, bg_out, cg_out, u_out, gate_out = outs

    pk = proj(OFF_K, KV_WIDTH)
    for hd in range(N_KV_HEADS):
        sl = slice(hd * HEAD_DIM, (hd + 1) * HEAD_DIM)
        k_out[0, :, sl] = _head_norm_rope(pk[:, sl], kg_ref[...], cos, sin, rope).astype(BF16)
    vt_out[0] = proj(OFF_V, KV_WIDTH).T.astype(BF16)
    if kv_only:
        return

    pq = proj(OFF_Q, ATTN_WIDTH)
    for hd in range(N_Q_HEADS):
        sl = slice(hd * HEAD_DIM, (hd + 1) * HEAD_DIM)
        qh = _head_norm_rope(pq[:, sl], qg_ref[...], cos, sin, rope)
        q_out[0, hd] = (qh * Q_SCALE_LOG2).astype(BF16)

    bg_out[0] = proj(OFF_SC, D_MODEL).astype(BF16)
    cg_out[0] = (proj(OFF_SC + D_MODEL, D_MODEL) * proj(OFF_SC + 2 * D_MODEL, D_MODEL)).astype(BF16)
    cf_a = proj(OFF_CF, D_MODEL)
    u_out[0] = (cf_a * jax.nn.sigmoid(proj(OFF_CF + D_MODEL, D_MODEL))).astype(BF16)
    for j in range(3):
        sl = slice(j * D_MODEL, (j + 1) * D_MODEL)
        gate_out[0, :, sl] = jax.nn.sigmoid(proj(OFF_GATE + j * D_MODEL, D_MODEL)).astype(BF16)


def _inproj(x, mod, g_norm, w, q_gain, k_gain, cos, sin, *, rope, kv_only, tm):
    bsz, seq, _ = x.shape
    tm = _row_tile(seq, tm)
    tok = lambda width: pl.BlockSpec((1, tm, width), lambda b, i: (b, i, 0))
    tok_shape = lambda width: jax.ShapeDtypeStruct((bsz, seq, width), BF16)
    out_specs = [tok(KV_WIDTH), pl.BlockSpec((1, KV_WIDTH, tm), lambda b, i: (b, 0, i))]
    out_shape = [tok_shape(KV_WIDTH), jax.ShapeDtypeStruct((bsz, KV_WIDTH, seq), BF16)]
    if not kv_only:
        q_spec = pl.BlockSpec((1, N_Q_HEADS, tm, HEAD_DIM), lambda b, i: (b, 0, i, 0))
        q_shape = jax.ShapeDtypeStruct((bsz, N_Q_HEADS, seq, HEAD_DIM), BF16)
        out_specs = [q_spec] + out_specs + [tok(D_MODEL)] * 3 + [tok(3 * D_MODEL)]
        out_shape = [q_shape] + out_shape + [tok_shape(D_MODEL)] * 3 + [tok_shape(3 * D_MODEL)]
    return pl.pallas_call(
        functools.partial(_inproj_kernel, rope=rope, kv_only=kv_only),
        grid=(bsz, seq // tm),
        in_specs=[
            pl.BlockSpec((1, tm, D_MODEL), lambda b, i: (b, i, 0)),
            _mod_spec(mod),
            _resident((1, D_MODEL)),
            _resident(w.shape),
            _resident((1, HEAD_DIM)),
            _resident((1, HEAD_DIM)),
            pl.BlockSpec((tm, HEAD_DIM), lambda b, i: (i, 0)),
            pl.BlockSpec((tm, HEAD_DIM), lambda b, i: (i, 0)),
        ],
        out_specs=out_specs,
        out_shape=out_shape,
        compiler_params=_params("parallel", "parallel"),
    )(x, mod, g_norm, w, q_gain, k_gain, cos, sin)


def _fill_ext(ext, prev_ref, main_ref, next_ref, first, last):
    rows = main_ref.shape[1]
    ext[0:HALO, :] = jnp.where(first, 0.0, prev_ref[0].astype(F32))
    ext[HALO:HALO + rows, :] = main_ref[0].astype(F32)
    ext[HALO + rows:2 * HALO + rows, :] = jnp.where(last, 0.0, next_ref[0].astype(F32))


def _dwconv(ext, shifted, wb_ref, taps, out_ref, zeros):
    rows, cols = out_ref.shape[1], out_ref.shape[2]
    offs = [HALO + k - taps // 2 for k in range(taps)]
    phases = sorted({o % SUBLANES for o in offs} - {0})
    n = rows + 2 * HALO - SUBLANES
    for i, r in enumerate(phases):
        shifted[i, 0:n, :] = ext[r:r + n, :]
    step = min(CONV_ROWS, rows)
    groups = step // SUBLANES
    slabs = [(r0, l0) for r0 in range(0, rows, step) for l0 in range(0, cols, LANES)]
    for s, (r0, l0) in enumerate(slabs):
        lanes = slice(l0, l0 + LANES)
        acc = jnp.broadcast_to(zeros[s * len(zeros) // len(slabs)][None],
                               (groups, SUBLANES, LANES))
        for k, o in enumerate(offs):
            r = o % SUBLANES
            src = ext if r == 0 else shifted.at[phases.index(r)]
            slab = src[r0 + o - r:r0 + o - r + step, lanes]
            acc = acc + slab.reshape(groups, SUBLANES, LANES) * wb_ref[k, :, lanes][None]
        out_ref[0, r0:r0 + step, lanes] = acc.reshape(step, LANES).astype(out_ref.dtype)


def _conv_side_job(cg_refs, u_refs, wscc_ref, wcfc_ref, cc_ref, cu_ref, scratch, zeros=None):
    ext_c, sh_c, ext_u, sh_u = scratch
    if zeros is None:
        zeros = [jnp.zeros((SUBLANES, LANES), F32)]
    i = pl.program_id(2)
    first = i == 0
    last = i == pl.num_programs(2) - 1
    _fill_ext(ext_c, cg_refs[1], cg_refs[0], cg_refs[2], first, last)
    _dwconv(ext_c, sh_c, wscc_ref, SC_KERNEL, cc_ref, zeros[:1])
    _fill_ext(ext_u, u_refs[1], u_refs[0], u_refs[2], first, last)
    _dwconv(ext_u, sh_u, wcfc_ref, CF_KERNEL, cu_ref, zeros)


def _scores_t(k_rows, qh):
    return lax.dot_general(k_rows, qh, (((1,), (1,)), ((), ())), preferred_element_type=F32)


def _attn_bounded_kernel(q_ref, k_ref, vt_ref, cg_ref, cgp_ref, cgn_ref, u_ref, up_ref, un_ref,
                         wscc_ref, wcfc_ref, o_ref, cc_ref, cu_ref, *scratch, kc):
    n_keys = k_ref.shape[1]
    tq = q_ref.shape[2]
    zeros = []
    for h in range(Q_PER_KV):
        qh = q_ref[0, h]
        acc = jnp.zeros((HEAD_DIM, tq), F32)
        lsum = jnp.zeros((SUBLANES, tq), F32)
        for k0 in range(0, n_keys, kc):
            p = jnp.exp2(_scores_t(k_ref[0, k0:k0 + kc, :], qh))
            zeros.append(jnp.minimum(p[0:SUBLANES, 0:LANES], 0.0))
            lsum = lsum + jnp.sum(p.reshape(kc // SUBLANES, SUBLANES, tq), axis=0)
            acc = acc + jnp.dot(vt_ref[0, :, k0:k0 + kc], p.astype(BF16),
                                preferred_element_type=F32)
        inv_l = 1.0 / jnp.sum(lsum, axis=0, keepdims=True)
        o_ref[0, h] = (acc * inv_l).T.astype(BF16)
    _conv_side_job((cg_ref, cgp_ref, cgn_ref), (u_ref, up_ref, un_ref), wscc_ref, wcfc_ref,
                   cc_ref, cu_ref, scratch, zeros[:len(zeros) * 3 // 4])


def _attn_online_kernel(q_ref, k_ref, vt_ref, cg_ref, cgp_ref, cgn_ref, u_ref, up_ref, un_ref,
                        wscc_ref, wcfc_ref, o_ref, cc_ref, cu_ref, *scratch, kc):
    n_keys = k_ref.shape[1]
    tq = q_ref.shape[2]

    def head(h, _):
        qh = q_ref[0, h]

        def chunk(c, carry):
            m, l, acc = carry
            k0 = pl.multiple_of(c * kc, kc)
            st = _scores_t(k_ref[0, pl.ds(k0, kc), :], qh)
            m_new = jnp.maximum(m, jnp.max(st, axis=0, keepdims=True))
            alpha = jnp.exp2(m - m_new)
            p = jnp.exp2(st - m_new)
            l = alpha * l + jnp.sum(p, axis=0, keepdims=True)
            acc = acc * alpha + jnp.dot(vt_ref[0, :, pl.ds(k0, kc)], p.astype(BF16),
                                        preferred_element_type=F32)
            return m_new, l, acc

        init = (jnp.full((1, tq), -jnp.inf, F32), jnp.zeros((1, tq), F32),
                jnp.zeros((HEAD_DIM, tq), F32))
        _, l, acc = lax.fori_loop(0, n_keys // kc, chunk, init)
        o_ref[0, h] = (acc / l).T.astype(BF16)
        return 0

    lax.fori_loop(0, Q_PER_KV, head, 0)
    _conv_side_job((cg_ref, cgp_ref, cgn_ref), (u_ref, up_ref, un_ref), wscc_ref, wcfc_ref,
                   cc_ref, cu_ref, scratch)


def _key_chunk(n_keys):
    for kc in (2816, 768, 1024, 512, 640, 256, 128):
        if n_keys % kc == 0:
            return kc
    raise ValueError(f"unsupported key count {n_keys}")


def _attention(q, k, vt, cg, u, wscc, wcfc, *, tq, bounded):
    bsz, _, seq, _ = q.shape
    n_keys = k.shape[1]
    tq = _row_tile(seq, tq)
    hb = tq // HALO
    n_halo = seq // HALO
    body = _attn_bounded_kernel if bounded else _attn_online_kernel
    qo_spec = pl.BlockSpec((1, Q_PER_KV, tq, HEAD_DIM), lambda b, g, i: (b, g, i, 0))
    main = pl.BlockSpec((1, tq, CONV_COLS), lambda b, g, i: (b, i, g))
    prev = pl.BlockSpec((1, HALO, CONV_COLS), lambda b, g, i: (b, jnp.maximum(i * hb - 1, 0), g))
    nxt = pl.BlockSpec((1, HALO, CONV_COLS),
                       lambda b, g, i: (b, jnp.minimum((i + 1) * hb, n_halo - 1), g))
    taps = lambda n: pl.BlockSpec((n, SUBLANES, CONV_COLS), lambda b, g, i: (0, 0, g))
    ext_rows = tq + 2 * HALO
    tok_shape = jax.ShapeDtypeStruct((bsz, seq, D_MODEL), BF16)
    return pl.pallas_call(
        functools.partial(body, kc=_key_chunk(n_keys)),
        grid=(bsz, N_KV_HEADS, seq // tq),
        in_specs=[
            qo_spec,
            pl.BlockSpec((1, n_keys, HEAD_DIM), lambda b, g, i: (b, 0, g)),
            pl.BlockSpec((1, HEAD_DIM, n_keys), lambda b, g, i: (b, g, 0)),
            main, prev, nxt,
            main, prev, nxt,
            taps(SC_KERNEL), taps(CF_KERNEL),
        ],
        out_specs=[qo_spec, main, main],
        out_shape=[jax.ShapeDtypeStruct((bsz, N_Q_HEADS, seq, HEAD_DIM), BF16), tok_shape, tok_shape],
        scratch_shapes=[pltpu.VMEM((ext_rows, CONV_COLS), F32),
                        pltpu.VMEM((2, ext_rows, CONV_COLS), F32),
                        pltpu.VMEM((ext_rows, CONV_COLS), F32),
                        pltpu.VMEM((SUBLANES - 1, ext_rows, CONV_COLS), F32)],
        compiler_params=_params("parallel", "parallel", "arbitrary"),
    )(q, k, vt, cg, cg, cg, u, u, u, wscc, wcfc)


def _attention_any(scores_bounded, *operands, tq):
    return lax.cond(scores_bounded,
                    functools.partial(_attention, tq=tq, bounded=True),
                    functools.partial(_attention, tq=tq, bounded=False),
                    *operands)


def _mix_kernel(x_ref, attn_ref, bg_ref, cc_ref, cu_ref, gate_ref, mod_ref,
                wao_ref, wsc_ref, wcf_ref, wo_ref, bcfc_ref, gln_ref, bln_ref, bcfo_ref, o_ref):
    dot = functools.partial(jnp.dot, preferred_element_type=F32)

    attn = jnp.concatenate([attn_ref[0, hd] for hd in range(N_Q_HEADS)], axis=-1)
    y_attn = dot(attn, wao_ref[...])
    y_sc = dot(bg_ref[0] * cc_ref[0], wsc_ref[...])

    uc = cu_ref[0].astype(F32) + bcfc_ref[...]
    mu = jnp.mean(uc, axis=-1, keepdims=True)
    xc = uc - mu
    un = xc * lax.rsqrt(jnp.mean(xc * xc, axis=-1, keepdims=True) + LN_EPS)
    un = un * gln_ref[...] + bln_ref[...]
    un = un * jax.nn.sigmoid(un)
    y_cf = dot(un.astype(BF16), wcf_ref[...]) + bcfo_ref[...]

    ga = gate_ref[0, :, 0:D_MODEL].astype(F32)
    gb = gate_ref[0, :, D_MODEL:2 * D_MODEL].astype(F32)
    gc = gate_ref[0, :, 2 * D_MODEL:3 * D_MODEL].astype(F32)
    merged = ga * y_attn + gb * y_sc + gc * y_cf
    y = dot(merged.astype(BF16), wo_ref[...])
    gt1 = mod_ref[0, :, 2 * D_MODEL:3 * D_MODEL]
    o_ref[0] = x_ref[0] + gt1 * y


def _mixer(x, attn, bg, cc, cu, gates, mod, lw, *, tm):
    bsz, seq, _ = x.shape
    tm = _row_tile(seq, tm)
    tok = lambda width: pl.BlockSpec((1, tm, width), lambda b, i: (b, i, 0))
    row = _resident((1, D_MODEL))
    sq = _resident((D_MODEL, D_MODEL))
    return pl.pallas_call(
        _mix_kernel,
        grid=(bsz, seq // tm),
        in_specs=[
            tok(D_MODEL),
            pl.BlockSpec((1, N_Q_HEADS, tm, HEAD_DIM), lambda b, i: (b, 0, i, 0)),
            tok(D_MODEL), tok(D_MODEL), tok(D_MODEL), tok(3 * D_MODEL),
            _mod_spec(mod),
            sq, sq, sq, sq,
            row, row, row, row,
        ],
        out_specs=tok(D_MODEL),
        out_shape=jax.ShapeDtypeStruct((bsz, seq, D_MODEL), F32),
        compiler_params=_params("parallel", "parallel"),
    )(x, attn, bg, cc, cu, gates, mod,
      lw["w_attn_out"], lw["w_sc_out"], lw["w_cf_out"], lw["w_o"],
      lw["b_cf_conv"], lw["g_cf_ln"], lw["b_cf_ln"], lw["b_cf_out"])


def _mlp_kernel(x_ref, mod_ref, g_ref, w1_ref, w2_ref, gf_ref, o_ref, *, final_norm):
    xt = x_ref[0]
    h = _modulated_rmsnorm(xt, g_ref[...], mod_ref[0, :, 3 * D_MODEL:4 * D_MODEL],
                           mod_ref[0, :, 4 * D_MODEL:5 * D_MODEL])
    hb = h.astype(BF16)
    acc = jnp.zeros(xt.shape, F32)
    for j in range(D_FF // D_MODEL):
        sl = slice(j * D_MODEL, (j + 1) * D_MODEL)
        hid = jnp.dot(hb, w1_ref[:, sl], preferred_element_type=F32)
        hid = jnp.square(jnp.maximum(hid, 0.0)).astype(BF16)
        acc = acc + jnp.dot(hid, w2_ref[sl, :], preferred_element_type=F32)
    y = xt + mod_ref[0, :, 5 * D_MODEL:6 * D_MODEL] * acc
    if final_norm:
        ms = jnp.mean(y * y, axis=-1, keepdims=True)
        y = y * lax.rsqrt(ms + NORM_EPS) * gf_ref[...]
    o_ref[0] = y


def _mlp(x, mod, g_norm, w1, w2, g_final, *, final_norm, tm):
    bsz, seq, _ = x.shape
    tm = _row_tile(seq, tm)
    tok = pl.BlockSpec((1, tm, D_MODEL), lambda b, i: (b, i, 0))
    return pl.pallas_call(
        functools.partial(_mlp_kernel, final_norm=final_norm),
        grid=(bsz, seq // tm),
        in_specs=[
            tok,
            _mod_spec(mod),
            _resident((1, D_MODEL)),
            _resident((D_MODEL, D_FF)),
            _resident((D_FF, D_MODEL)),
            _resident((1, D_MODEL)),
        ],
        out_specs=tok,
        out_shape=jax.ShapeDtypeStruct((bsz, seq, D_MODEL), F32),
        compiler_params=_params("parallel", "parallel"),
    )(x, mod, g_norm, w1, w2, g_final)


def _rope_tables(n_tokens):
    rows = n_tokens // GRID_W
    row = jnp.repeat(jnp.arange(rows), GRID_W)
    col = jnp.tile(jnp.arange(GRID_W), rows)
    pos = jnp.stack([row, col], axis=-1).astype(F32)
    inv_freq = ROPE_THETA ** (-jnp.arange(ROPE_FREQS, dtype=F32) * 2.0 / ROPE_AXIS_DIM)
    ang = pos[:, :, None] * inv_freq
    cos, sin = jnp.cos(ang), jnp.sin(ang)
    cos_t = jnp.concatenate([cos, cos], axis=-1).reshape(n_tokens, HEAD_DIM)
    sin_t = jnp.concatenate([-sin, sin], axis=-1).reshape(n_tokens, HEAD_DIM)
    return cos_t, sin_t


def kernel(x, c, ctx, c_ctx, w_mod, b_mod, g_norm1, g_norm2, w_in, q_gain, k_gain, w_attn_out,
           w_sc_conv, w_sc_out, w_cf_conv, b_cf_conv, g_cf_ln, b_cf_ln, w_cf_out, b_cf_out,
           w_o, w_mlp_in, w_mlp_out, g_final):
    bsz, seq, _ = x.shape
    n_ctx = ctx.shape[1]
    depth = w_mod.shape[0]
    assert bsz + 1 <= MOD_ROWS and seq % GRID_W == 0
    row = lambda v: v.reshape(1, -1)
    taps = lambda w: jnp.broadcast_to(w[:, None, :], (w.shape[0], SUBLANES, D_MODEL))

    cond = jnp.zeros((MOD_ROWS, D_MODEL), F32).at[:bsz].set(c).at[bsz].set(c_ctx)
    mod_all = _modulation(cond, w_mod, b_mod)
    cos_lat, sin_lat = _rope_tables(seq)
    cos_ctx, sin_ctx = cos_lat[:n_ctx], sin_lat[:n_ctx]

    for l in range(depth):
        last = l == depth - 1
        mod_lat = mod_all[l, :bsz][:, None, :]
        mod_ctx = mod_all[l, bsz:bsz + 1][:, None, :]
        w_in_l = w_in[l].astype(BF16)
        lw = dict(
            w_attn_out=w_attn_out[l].astype(BF16), w_sc_out=w_sc_out[l].astype(BF16),
            w_cf_out=w_cf_out[l].astype(BF16), w_o=w_o[l].astype(BF16),
            b_cf_conv=row(b_cf_conv[l]), g_cf_ln=row(g_cf_ln[l]), b_cf_ln=row(b_cf_ln[l]),
            b_cf_out=row(b_cf_out[l]))
        conv_w = (taps(w_sc_conv[l]), taps(w_cf_conv[l]))
        w1 = w_mlp_in[l].astype(BF16)
        w2 = w_mlp_out[l].astype(BF16)
        gains = (row(q_gain[l]), row(k_gain[l]))
        score_bound = HEAD_DIM ** 0.5 * jnp.max(jnp.abs(q_gain[l])) * jnp.max(jnp.abs(k_gain[l]))
        bounded = score_bound <= SAFE_SCORE_BOUND

        if last:
            k_ctx, vt_ctx = _inproj(ctx, mod_ctx, row(g_norm1[l]), w_in_l[:, OFF_K:OFF_SC], *gains,
                                    cos_ctx, sin_ctx, rope=False, kv_only=True, tm=256)
        else:
            q_c, k_ctx, vt_ctx, bg_c, cg_c, u_c, gates_c = _inproj(
                ctx, mod_ctx, row(g_norm1[l]), w_in_l, *gains, cos_ctx, sin_ctx,
                rope=False, kv_only=False, tm=256)

        q, k_lat, vt_lat, bg, cg, u, gates = _inproj(
            x, mod_lat, row(g_norm1[l]), w_in_l, *gains, cos_lat, sin_lat,
            rope=True, kv_only=False, tm=512)
        k_all = jnp.concatenate([k_ctx, k_lat], axis=1)
        vt_all = jnp.concatenate([vt_ctx, vt_lat], axis=2)
        attn, cc, cu = _attention_any(bounded, q, k_all, vt_all, cg, u, *conv_w, tq=256)
        x = _mixer(x, attn, bg, cc, cu, gates, mod_lat, lw, tm=512)
        x = _mlp(x, mod_lat, row(g_norm2[l]), w1, w2, row(g_final), final_norm=last, tm=512)

        if not last:
            attn_c, cc_c, cu_c = _attention_any(bounded, q_c, k_ctx, vt_ctx, cg_c, u_c, *conv_w,
                                                tq=256)
            ctx = _mixer(ctx, attn_c, bg_c, cc_c, cu_c, gates_c, mod_ctx, lw, tm=256)
            ctx = _mlp(ctx, mod_ctx, row(g_norm2[l]), w1, w2, row(g_final), final_norm=False, tm=256)
    return x
```

```python
import functools

import jax
import jax.numpy as jnp
from jax import lax
from jax.experimental import pallas as pl
from jax.experimental.pallas import tpu as pltpu

D_MODEL = 1024
GRID_W = 64
HEAD_DIM = 128
N_Q_HEADS = 8
N_KV_HEADS = 2
Q_PER_KV = N_Q_HEADS // N_KV_HEADS
ATTN_WIDTH = N_Q_HEADS * HEAD_DIM
KV_WIDTH = N_KV_HEADS * HEAD_DIM
ROPE_THETA = 10000.0
ROPE_AXIS_DIM = HEAD_DIM // 2
ROPE_FREQS = ROPE_AXIS_DIM // 2
SC_KERNEL = 3
CF_KERNEL = 31
D_FF = 4 * D_MODEL
NORM_EPS = 1e-6
LN_EPS = 1e-5
LOG2E = 1.4426950408889634
Q_SCALE_LOG2 = HEAD_DIM ** -0.5 * LOG2E
SAFE_SCORE_BOUND = 60.0
OFF_Q = 0
OFF_K = OFF_Q + ATTN_WIDTH
OFF_V = OFF_K + KV_WIDTH
OFF_SC = OFF_V + KV_WIDTH
OFF_CF = OFF_SC + 3 * D_MODEL
OFF_GATE = OFF_CF + 2 * D_MODEL
D_IN = OFF_GATE + 3 * D_MODEL

V7X_VMEM_BYTES = 64 * 1024 * 1024
VMEM_LIMIT_BYTES = V7X_VMEM_BYTES - 8 * 1024 * 1024
SUBLANES = 8
LANES = 128
HALO = 16
MOD_ROWS = 16
CONV_ROWS = 64
CONV_COLS = D_MODEL // N_KV_HEADS
SUB_ROWS = 256
Q_COLS = 256
BOUNDED_KEY_CHUNK = 2816
F32 = jnp.float32
BF16 = jnp.bfloat16


def _params(*sem, flags=None):
    return pltpu.CompilerParams(dimension_semantics=sem, vmem_limit_bytes=VMEM_LIMIT_BYTES,
                                flags=flags)


def _resident(shape):
    nd = len(shape)
    return pl.BlockSpec(shape, lambda *_: (0,) * nd, pipeline_mode=pl.Buffered(1))


def _mod_spec(mod):
    if mod.shape[0] > 1:
        return pl.BlockSpec((1, 1, 6 * D_MODEL), lambda b, *_: (b, 0, 0))
    return pl.BlockSpec((1, 1, 6 * D_MODEL), lambda *_: (0, 0, 0))


def _row_tile(seq, want):
    t = min(want, seq)
    assert seq % t == 0
    return t


def _mod_kernel(c_ref, w_ref, b_ref, o_ref):
    a = c_ref[...]
    a = a * jax.nn.sigmoid(a)
    o_ref[0] = jnp.dot(a, w_ref[0], preferred_element_type=F32,
                       precision=lax.Precision.HIGHEST) + b_ref[0]


def _modulation(cond, w_mod, b_mod):
    n_layers = w_mod.shape[0]
    tn = D_MODEL
    return pl.pallas_call(
        _mod_kernel,
        grid=(n_layers, 6 * D_MODEL // tn),
        in_specs=[
            pl.BlockSpec((MOD_ROWS, D_MODEL), lambda l, j: (0, 0)),
            pl.BlockSpec((1, D_MODEL, tn), lambda l, j: (l, 0, j)),
            pl.BlockSpec((1, 1, tn), lambda l, j: (l, 0, j)),
        ],
        out_specs=pl.BlockSpec((1, MOD_ROWS, tn), lambda l, j: (l, 0, j)),
        out_shape=jax.ShapeDtypeStruct((n_layers, MOD_ROWS, 6 * D_MODEL), F32),
        compiler_params=_params("arbitrary", "arbitrary"),
    )(cond, w_mod, b_mod[:, None, :])


def _modulated_rmsnorm(xt, g, shift, scale):
    ms = jnp.mean(xt * xt, axis=-1, keepdims=True)
    h = xt * lax.rsqrt(ms + NORM_EPS) * g
    return h * (1.0 + scale) + shift


def _head_norm_rope(ph, gain, cos, sin, rope):
    ms = jnp.mean(ph * ph, axis=-1, keepdims=True)
    y = ph * lax.rsqrt(ms + NORM_EPS) * gain
    if rope:
        lane = lax.broadcasted_iota(jnp.int32, y.shape, 1)
        first_half = (lane % (2 * ROPE_FREQS)) < ROPE_FREQS
        partner = jnp.where(first_half,
                            pltpu.roll(y, HEAD_DIM - ROPE_FREQS, 1),
                            pltpu.roll(y, ROPE_FREQS, 1))
        y = y * cos + partner * sin
    return y


def _inproj_kernel(x_ref, mod_ref, g_ref, w_ref, qg_ref, kg_ref, cos_ref, sin_ref, *outs,
                   rope, kv_only):
    tm = x_ref.shape[1]
    sub = min(SUB_ROWS, tm)
    for r0 in range(0, tm, sub):
        _inproj_rows(slice(r0, r0 + sub), x_ref, mod_ref, g_ref, w_ref, qg_ref, kg_ref,
                     cos_ref, sin_ref, outs, rope, kv_only)


def _inproj_rows(rows, x_ref, mod_ref, g_ref, w_ref, qg_ref, kg_ref, cos_ref, sin_ref, outs,
                 rope, kv_only):
    xt = x_ref[0, rows, :]
    h = _modulated_rmsnorm(xt, g_ref[...], mod_ref[0, :, 0:D_MODEL],
                           mod_ref[0, :, D_MODEL:2 * D_MODEL])
    hb = h.astype(BF16)
    base = OFF_K if kv_only else 0

    def proj(off, n):
        return jnp.dot(hb, w_ref[:, off - base:off - base + n], preferred_element_type=F32)

    cos = cos_ref[rows, :] if rope else None
    sin = sin_ref[rows, :] if rope else None
    if kv_only:
        k_out, vt_out = outs
    else:
        q_out, k_out, vt_out, bg_out, cg_out, u_out, gate_out = outs

    pk = proj(OFF_K, KV_WIDTH)
    for hd in range(N_KV_HEADS):
        sl = slice(hd * HEAD_DIM, (hd + 1) * HEAD_DIM)
        k_out[0, rows, sl] = _head_norm_rope(pk[:, sl], kg_ref[...], cos, sin, rope).astype(BF16)
    vt_out[0, :, rows] = proj(OFF_V, KV_WIDTH).T.astype(BF16)
    if kv_only:
        return

    pq = proj(OFF_Q, ATTN_WIDTH)
    for hd in range(N_Q_HEADS):
        sl = slice(hd * HEAD_DIM, (hd + 1) * HEAD_DIM)
        qh = _head_norm_rope(pq[:, sl], qg_ref[...], cos, sin, rope)
        q_out[0, hd, rows, :] = (qh * Q_SCALE_LOG2).astype(BF16)

    bg_out[0, rows, :] = proj(OFF_SC, D_MODEL).astype(BF16)
    cg_out[0, rows, :] = (proj(OFF_SC + D_MODEL, D_MODEL)
                          * proj(OFF_SC + 2 * D_MODEL, D_MODEL)).astype(BF16)
    cf_a = proj(OFF_CF, D_MODEL)
    u_out[0, rows, :] = (cf_a * jax.nn.sigmoid(proj(OFF_CF + D_MODEL, D_MODEL))).astype(BF16)
    for j in range(3):
        sl = slice(j * D_MODEL, (j + 1) * D_MODEL)
        gate_out[0, rows, sl] = jax.nn.sigmoid(proj(OFF_GATE + j * D_MODEL, D_MODEL)).astype(BF16)


def _inproj(x, mod, g_norm, w, q_gain, k_gain, cos, sin, *, rope, kv_only, tm):
    bsz, seq, _ = x.shape
    tm = _row_tile(seq, tm)
    tok = lambda width: pl.BlockSpec((1, tm, width), lambda b, i: (b, i, 0))
    tok_shape = lambda width: jax.ShapeDtypeStruct((bsz, seq, width), BF16)
    out_specs = [tok(KV_WIDTH), pl.BlockSpec((1, KV_WIDTH, tm), lambda b, i: (b, 0, i))]
    out_shape = [tok_shape(KV_WIDTH), jax.ShapeDtypeStruct((bsz, KV_WIDTH, seq), BF16)]
    if not kv_only:
        q_spec = pl.BlockSpec((1, N_Q_HEADS, tm, HEAD_DIM), lambda b, i: (b, 0, i, 0))
        q_shape = jax.ShapeDtypeStruct((bsz, N_Q_HEADS, seq, HEAD_DIM), BF16)
        out_specs = [q_spec] + out_specs + [tok(D_MODEL)] * 3 + [tok(3 * D_MODEL)]
        out_shape = [q_shape] + out_shape + [tok_shape(D_MODEL)] * 3 + [tok_shape(3 * D_MODEL)]
    return pl.pallas_call(
        functools.partial(_inproj_kernel, rope=rope, kv_only=kv_only),
        grid=(bsz, seq // tm),
        in_specs=[
            pl.BlockSpec((1, tm, D_MODEL), lambda b, i: (b, i, 0)),
            _mod_spec(mod),
            _resident((1, D_MODEL)),
            _resident(w.shape),
            _resident((1, HEAD_DIM)),
            _resident((1, HEAD_DIM)),
            pl.BlockSpec((tm, HEAD_DIM), lambda b, i: (i, 0)),
            pl.BlockSpec((tm, HEAD_DIM), lambda b, i: (i, 0)),
        ],
        out_specs=out_specs,
        out_shape=out_shape,
        compiler_params=_params("parallel", "parallel"),
    )(x, mod, g_norm, w, q_gain, k_gain, cos, sin)


def _fill_ext(ext, prev_ref, main_ref, next_ref, first, last):
    rows = main_ref.shape[1]
    ext[0:HALO, :] = jnp.where(first, 0.0, prev_ref[0].astype(F32))
    ext[HALO:HALO + rows, :] = main_ref[0].astype(F32)
    ext[HALO + rows:2 * HALO + rows, :] = jnp.where(last, 0.0, next_ref[0].astype(F32))


def _dwconv(ext, shifted, wb_ref, taps, out_ref, zeros):
    rows, cols = out_ref.shape[1], out_ref.shape[2]
    offs = [HALO + k - taps // 2 for k in range(taps)]
    phases = sorted({o % SUBLANES for o in offs} - {0})
    n = rows + 2 * HALO - SUBLANES
    for i, r in enumerate(phases):
        shifted[i, 0:n, :] = ext[r:r + n, :]
    step = min(CONV_ROWS, rows)
    groups = step // SUBLANES
    slabs = [(r0, l0) for r0 in range(0, rows, step) for l0 in range(0, cols, LANES)]
    for s, (r0, l0) in enumerate(slabs):
        lanes = slice(l0, l0 + LANES)
        acc = jnp.broadcast_to(zeros[s * len(zeros) // len(slabs)][None],
                               (groups, SUBLANES, LANES))
        for k, o in enumerate(offs):
            r = o % SUBLANES
            src = ext if r == 0 else shifted.at[phases.index(r)]
            slab = src[r0 + o - r:r0 + o - r + step, lanes]
            acc = acc + slab.reshape(groups, SUBLANES, LANES) * wb_ref[k, :, lanes][None]
        out_ref[0, r0:r0 + step, lanes] = acc.reshape(step, LANES).astype(out_ref.dtype)


def _conv_side_job(cg_refs, u_refs, wscc_ref, wcfc_ref, cc_ref, cu_ref, scratch, zeros=None):
    ext_c, sh_c, ext_u, sh_u = scratch
    if zeros is None:
        zeros = [jnp.zeros((SUBLANES, LANES), F32)]
    i = pl.program_id(2)
    first = i == 0
    last = i == pl.num_programs(2) - 1
    _fill_ext(ext_c, cg_refs[1], cg_refs[0], cg_refs[2], first, last)
    _dwconv(ext_c, sh_c, wscc_ref, SC_KERNEL, cc_ref, zeros[:1])
    _fill_ext(ext_u, u_refs[1], u_refs[0], u_refs[2], first, last)
    _dwconv(ext_u, sh_u, wcfc_ref, CF_KERNEL, cu_ref, zeros)


def _scores_t(k_rows, qh):
    return lax.dot_general(k_rows, qh, (((1,), (1,)), ((), ())), preferred_element_type=F32)


def _attn_bounded_kernel(q_ref, k_ref, vt_ref, cg_ref, cgp_ref, cgn_ref, u_ref, up_ref, un_ref,
                         wscc_ref, wcfc_ref, o_ref, cc_ref, cu_ref, *scratch, kc):
    n_keys = k_ref.shape[1]
    tq = min(Q_COLS, q_ref.shape[2])
    zeros = []
    for t0, h in [(t0, h) for t0 in range(0, q_ref.shape[2], tq) for h in range(Q_PER_KV)]:
        qh = q_ref[0, h, t0:t0 + tq, :]
        acc = jnp.zeros((HEAD_DIM, tq), F32)
        lsum = jnp.zeros((SUBLANES, tq), F32)
        for k0, k1 in _key_chunks(n_keys, kc):
            p = jnp.exp2(_scores_t(k_ref[0, k0:k1, :], qh))
            zeros.append(jnp.minimum(p[0:SUBLANES, 0:LANES], 0.0))
            lsum = lsum + jnp.sum(p.reshape((k1 - k0) // SUBLANES, SUBLANES, tq), axis=0)
            acc = acc + jnp.dot(vt_ref[0, :, k0:k1], p.astype(BF16),
                                preferred_element_type=F32)
        inv_l = 1.0 / jnp.sum(lsum, axis=0, keepdims=True)
        o_ref[0, h, t0:t0 + tq, :] = (acc * inv_l).T.astype(BF16)
    _conv_side_job((cg_ref, cgp_ref, cgn_ref), (u_ref, up_ref, un_ref), wscc_ref, wcfc_ref,
                   cc_ref, cu_ref, scratch, zeros[:len(zeros) * 3 // 4])


def _attn_online_kernel(q_ref, k_ref, vt_ref, cg_ref, cgp_ref, cgn_ref, u_ref, up_ref, un_ref,
                        wscc_ref, wcfc_ref, o_ref, cc_ref, cu_ref, *scratch, kc):
    n_keys = k_ref.shape[1]
    tq = min(Q_COLS, q_ref.shape[2])

    def head(i, _):
        h = i % Q_PER_KV
        t0 = pl.multiple_of((i // Q_PER_KV) * tq, tq)
        qh = q_ref[0, h, pl.ds(t0, tq), :]

        def chunk(c, carry):
            m, l, acc = carry
            k0 = pl.multiple_of(c * kc, kc)
            st = _scores_t(k_ref[0, pl.ds(k0, kc), :], qh)
            m_new = jnp.maximum(m, jnp.max(st, axis=0, keepdims=True))
            alpha = jnp.exp2(m - m_new)
            p = jnp.exp2(st - m_new)
            l = alpha * l + jnp.sum(p, axis=0, keepdims=True)
            acc = acc * alpha + jnp.dot(vt_ref[0, :, pl.ds(k0, kc)], p.astype(BF16),
                                        preferred_element_type=F32)
            return m_new, l, acc

        init = (jnp.full((1, tq), -jnp.inf, F32), jnp.zeros((1, tq), F32),
                jnp.zeros((HEAD_DIM, tq), F32))
        _, l, acc = lax.fori_loop(0, n_keys // kc, chunk, init)
        o_ref[0, h, pl.ds(t0, tq), :] = (acc / l).T.astype(BF16)
        return 0

    lax.fori_loop(0, Q_PER_KV * (q_ref.shape[2] // tq), head, 0)
    _conv_side_job((cg_ref, cgp_ref, cgn_ref), (u_ref, up_ref, un_ref), wscc_ref, wcfc_ref,
                   cc_ref, cu_ref, scratch)


def _key_chunk(n_keys):
    for kc in (768, 1024, 512, 640, 256, 128):
        if n_keys % kc == 0:
            return kc
    raise ValueError(f"unsupported key count {n_keys}")


def _key_chunks(n_keys, kc):
    rem = n_keys % kc
    return ([(0, rem)] if rem else []) + [(k0, k0 + kc) for k0 in range(rem, n_keys, kc)]


def _attention(q, k, vt, cg, u, wscc, wcfc, *, tq, bounded):
    bsz, _, seq, _ = q.shape
    n_keys = k.shape[1]
    tq = _row_tile(seq, tq)
    hb = tq // HALO
    n_halo = seq // HALO
    body = _attn_bounded_kernel if bounded else _attn_online_kernel
    qo_spec = pl.BlockSpec((1, Q_PER_KV, tq, HEAD_DIM), lambda b, g, i: (b, g, i, 0))
    main = pl.BlockSpec((1, tq, CONV_COLS), lambda b, g, i: (b, i, g))
    prev = pl.BlockSpec((1, HALO, CONV_COLS), lambda b, g, i: (b, jnp.maximum(i * hb - 1, 0), g))
    nxt = pl.BlockSpec((1, HALO, CONV_COLS),
                       lambda b, g, i: (b, jnp.minimum((i + 1) * hb, n_halo - 1), g))
    taps = lambda n: pl.BlockSpec((n, SUBLANES, CONV_COLS), lambda b, g, i: (0, 0, g))
    ext_rows = tq + 2 * HALO
    tok_shape = jax.ShapeDtypeStruct((bsz, seq, D_MODEL), BF16)
    return pl.pallas_call(
        functools.partial(body, kc=BOUNDED_KEY_CHUNK if bounded else _key_chunk(n_keys)),
        grid=(bsz, N_KV_HEADS, seq // tq),
        in_specs=[
            qo_spec,
            pl.BlockSpec((1, n_keys, HEAD_DIM), lambda b, g, i: (b, 0, g)),
            pl.BlockSpec((1, HEAD_DIM, n_keys), lambda b, g, i: (b, g, 0)),
            main, prev, nxt,
            main, prev, nxt,
            taps(SC_KERNEL), taps(CF_KERNEL),
        ],
        out_specs=[qo_spec, main, main],
        out_shape=[jax.ShapeDtypeStruct((bsz, N_Q_HEADS, seq, HEAD_DIM), BF16), tok_shape, tok_shape],
        scratch_shapes=[pltpu.VMEM((ext_rows, CONV_COLS), F32),
                        pltpu.VMEM((2, ext_rows, CONV_COLS), F32),
                        pltpu.VMEM((ext_rows, CONV_COLS), F32),
                        pltpu.VMEM((SUBLANES - 1, ext_rows, CONV_COLS), F32)],
        compiler_params=_params("parallel", "parallel", "arbitrary"),
    )(q, k, vt, cg, cg, cg, u, u, u, wscc, wcfc)


def _attention_any(scores_bounded, *operands, tq):
    return lax.cond(scores_bounded,
                    functools.partial(_attention, tq=tq, bounded=True),
                    functools.partial(_attention, tq=tq, bounded=False),
                    *operands)


def _mix_kernel(x_ref, attn_ref, bg_ref, cc_ref, cu_ref, gate_ref, mod_ref,
                wao_ref, wsc_ref, wcf_ref, wo_ref, bcfc_ref, gln_ref, bln_ref, bcfo_ref, o_ref):
    dot = functools.partial(jnp.dot, preferred_element_type=F32)
    tm = x_ref.shape[1]
    sub = min(SUB_ROWS, tm)
    for r0 in range(0, tm, sub):
        rows = slice(r0, r0 + sub)
        attn = jnp.concatenate([attn_ref[0, hd, rows, :] for hd in range(N_Q_HEADS)], axis=-1)
        y_attn = dot(attn, wao_ref[...])
        y_sc = dot(bg_ref[0, rows, :] * cc_ref[0, rows, :], wsc_ref[...])

        uc = cu_ref[0, rows, :].astype(F32) + bcfc_ref[...]
        mu = jnp.mean(uc, axis=-1, keepdims=True)
        xc = uc - mu
        un = xc * lax.rsqrt(jnp.mean(xc * xc, axis=-1, keepdims=True) + LN_EPS)
        un = un * gln_ref[...] + bln_ref[...]
        un = un * jax.nn.sigmoid(un)
        y_cf = dot(un.astype(BF16), wcf_ref[...]) + bcfo_ref[...]

        ga = gate_ref[0, rows, 0:D_MODEL].astype(F32)
        gb = gate_ref[0, rows, D_MODEL:2 * D_MODEL].astype(F32)
        gc = gate_ref[0, rows, 2 * D_MODEL:3 * D_MODEL].astype(F32)
        merged = ga * y_attn + gb * y_sc + gc * y_cf
        y = dot(merged.astype(BF16), wo_ref[...])
        gt1 = mod_ref[0, :, 2 * D_MODEL:3 * D_MODEL]
        o_ref[0, rows, :] = x_ref[0, rows, :] + gt1 * y


def _mixer(x, attn, bg, cc, cu, gates, mod, lw, *, tm):
    bsz, seq, _ = x.shape
    tm = _row_tile(seq, tm)
    tok = lambda width: pl.BlockSpec((1, tm, width), lambda b, i: (b, i, 0))
    row = _resident((1, D_MODEL))
    sq = _resident((D_MODEL, D_MODEL))
    return pl.pallas_call(
        _mix_kernel,
        grid=(bsz, seq // tm),
        in_specs=[
            tok(D_MODEL),
            pl.BlockSpec((1, N_Q_HEADS, tm, HEAD_DIM), lambda b, i: (b, 0, i, 0)),
            tok(D_MODEL), tok(D_MODEL), tok(D_MODEL), tok(3 * D_MODEL),
            _mod_spec(mod),
            sq, sq, sq, sq,
            row, row, row, row,
        ],
        out_specs=tok(D_MODEL),
        out_shape=jax.ShapeDtypeStruct((bsz, seq, D_MODEL), F32),
        compiler_params=_params("parallel", "parallel"),
    )(x, attn, bg, cc, cu, gates, mod,
      lw["w_attn_out"], lw["w_sc_out"], lw["w_cf_out"], lw["w_o"],
      lw["b_cf_conv"], lw["g_cf_ln"], lw["b_cf_ln"], lw["b_cf_out"])


def _mlp_kernel(x_ref, mod_ref, g_ref, w1_ref, w2_ref, gf_ref, o_ref, *, final_norm):
    tm = x_ref.shape[1]
    sub = min(SUB_ROWS, tm)
    for r0 in range(0, tm, sub):
        rows = slice(r0, r0 + sub)
        xt = x_ref[0, rows, :]
        h = _modulated_rmsnorm(xt, g_ref[...], mod_ref[0, :, 3 * D_MODEL:4 * D_MODEL],
                               mod_ref[0, :, 4 * D_MODEL:5 * D_MODEL])
        hb = h.astype(BF16)
        acc = jnp.zeros(xt.shape, F32)
        for j in range(D_FF // D_MODEL):
            sl = slice(j * D_MODEL, (j + 1) * D_MODEL)
            hid = jnp.dot(hb, w1_ref[:, sl], preferred_element_type=F32)
            hid = jnp.square(jnp.maximum(hid, 0.0)).astype(BF16)
            acc = acc + jnp.dot(hid, w2_ref[sl, :], preferred_element_type=F32)
        y = xt + mod_ref[0, :, 5 * D_MODEL:6 * D_MODEL] * acc
        if final_norm:
            ms = jnp.mean(y * y, axis=-1, keepdims=True)
            y = y * lax.rsqrt(ms + NORM_EPS) * gf_ref[...]
        o_ref[0, rows, :] = y


def _mlp(x, mod, g_norm, w1, w2, g_final, *, final_norm, tm):
    bsz, seq, _ = x.shape
    tm = _row_tile(seq, tm)
    tok = pl.BlockSpec((1, tm, D_MODEL), lambda b, i: (b, i, 0))
    return pl.pallas_call(
        functools.partial(_mlp_kernel, final_norm=final_norm),
        grid=(bsz, seq // tm),
        in_specs=[
            tok,
            _mod_spec(mod),
            _resident((1, D_MODEL)),
            _resident((D_MODEL, D_FF)),
            _resident((D_FF, D_MODEL)),
            _resident((1, D_MODEL)),
        ],
        out_specs=tok,
        out_shape=jax.ShapeDtypeStruct((bsz, seq, D_MODEL), F32),
        compiler_params=_params("parallel", "parallel"),
    )(x, mod, g_norm, w1, w2, g_final)


def _rope_tables(n_tokens):
    rows = n_tokens // GRID_W
    row = jnp.repeat(jnp.arange(rows), GRID_W)
    col = jnp.tile(jnp.arange(GRID_W), rows)
    pos = jnp.stack([row, col], axis=-1).astype(F32)
    inv_freq = ROPE_THETA ** (-jnp.arange(ROPE_FREQS, dtype=F32) * 2.0 / ROPE_AXIS_DIM)
    ang = pos[:, :, None] * inv_freq
    cos, sin = jnp.cos(ang), jnp.sin(ang)
    cos_t = jnp.concatenate([cos, cos], axis=-1).reshape(n_tokens, HEAD_DIM)
    sin_t = jnp.concatenate([-sin, sin], axis=-1).reshape(n_tokens, HEAD_DIM)
    return cos_t, sin_t


def kernel(x, c, ctx, c_ctx, w_mod, b_mod, g_norm1, g_norm2, w_in, q_gain, k_gain, w_attn_out,
           w_sc_conv, w_sc_out, w_cf_conv, b_cf_conv, g_cf_ln, b_cf_ln, w_cf_out, b_cf_out,
           w_o, w_mlp_in, w_mlp_out, g_final):
    bsz, seq, _ = x.shape
    n_ctx = ctx.shape[1]
    depth = w_mod.shape[0]
    assert bsz + 1 <= MOD_ROWS and seq % GRID_W == 0
    row = lambda v: v.reshape(1, -1)
    taps = lambda w: jnp.broadcast_to(w[:, None, :], (w.shape[0], SUBLANES, D_MODEL))

    cond = jnp.zeros((MOD_ROWS, D_MODEL), F32).at[:bsz].set(c).at[bsz].set(c_ctx)
    mod_all = _modulation(cond, w_mod, b_mod)
    cos_lat, sin_lat = _rope_tables(seq)
    cos_ctx, sin_ctx = cos_lat[:n_ctx], sin_lat[:n_ctx]

    for l in range(depth):
        last = l == depth - 1
        mod_lat = mod_all[l, :bsz][:, None, :]
        mod_ctx = mod_all[l, bsz:bsz + 1][:, None, :]
        w_in_l = w_in[l].astype(BF16)
        lw = dict(
            w_attn_out=w_attn_out[l].astype(BF16), w_sc_out=w_sc_out[l].astype(BF16),
            w_cf_out=w_cf_out[l].astype(BF16), w_o=w_o[l].astype(BF16),
            b_cf_conv=row(b_cf_conv[l]), g_cf_ln=row(g_cf_ln[l]), b_cf_ln=row(b_cf_ln[l]),
            b_cf_out=row(b_cf_out[l]))
        conv_w = (taps(w_sc_conv[l]), taps(w_cf_conv[l]))
        w1 = w_mlp_in[l].astype(BF16)
        w2 = w_mlp_out[l].astype(BF16)
        gains = (row(q_gain[l]), row(k_gain[l]))
        score_bound = HEAD_DIM ** 0.5 * jnp.max(jnp.abs(q_gain[l])) * jnp.max(jnp.abs(k_gain[l]))
        bounded = score_bound <= SAFE_SCORE_BOUND

        if last:
            k_ctx, vt_ctx = _inproj(ctx, mod_ctx, row(g_norm1[l]), w_in_l[:, OFF_K:OFF_SC], *gains,
                                    cos_ctx, sin_ctx, rope=False, kv_only=True, tm=256)
        else:
            q_c, k_ctx, vt_ctx, bg_c, cg_c, u_c, gates_c = _inproj(
                ctx, mod_ctx, row(g_norm1[l]), w_in_l, *gains, cos_ctx, sin_ctx,
                rope=False, kv_only=False, tm=256)

        q, k_lat, vt_lat, bg, cg, u, gates = _inproj(
            x, mod_lat, row(g_norm1[l]), w_in_l, *gains, cos_lat, sin_lat,
            rope=True, kv_only=False, tm=512)
        k_all = jnp.concatenate([k_ctx, k_lat], axis=1)
        vt_all = jnp.concatenate([vt_ctx, vt_lat], axis=2)
        attn, cc, cu = _attention_any(bounded, q, k_all, vt_all, cg, u, *conv_w, tq=512)
        x = _mixer(x, attn, bg, cc, cu, gates, mod_lat, lw, tm=512)
        x = _mlp(x, mod_lat, row(g_norm2[l]), w1, w2, row(g_final), final_norm=last, tm=512)

        if not last:
            attn_c, cc_c, cu_c = _attention_any(bounded, q_c, k_ctx, vt_ctx, cg_c, u_c, *conv_w,
                                                tq=256)
            ctx = _mixer(ctx, attn_c, bg_c, cc_c, cu_c, gates_c, mod_ctx, lw, tm=256)
            ctx = _mlp(ctx, mod_ctx, row(g_norm2[l]), w1, w2, row(g_final), final_norm=False, tm=256)
    return x
```

```python
import functools

import jax
import jax.numpy as jnp
from jax import lax
from jax.experimental import pallas as pl
from jax.experimental.pallas import tpu as pltpu

D_MODEL = 1024
GRID_W = 64
HEAD_DIM = 128
N_Q_HEADS = 8
N_KV_HEADS = 2
Q_PER_KV = N_Q_HEADS // N_KV_HEADS
ATTN_WIDTH = N_Q_HEADS * HEAD_DIM
KV_WIDTH = N_KV_HEADS * HEAD_DIM
ROPE_THETA = 10000.0
ROPE_AXIS_DIM = HEAD_DIM // 2
ROPE_FREQS = ROPE_AXIS_DIM // 2
SC_KERNEL = 3
CF_KERNEL = 31
D_FF = 4 * D_MODEL
NORM_EPS = 1e-6
LN_EPS = 1e-5
LOG2E = 1.4426950408889634
Q_SCALE_LOG2 = HEAD_DIM ** -0.5 * LOG2E
SAFE_SCORE_BOUND = 60.0
OFF_Q = 0
OFF_K = OFF_Q + ATTN_WIDTH
OFF_V = OFF_K + KV_WIDTH
OFF_SC = OFF_V + KV_WIDTH
OFF_CF = OFF_SC + 3 * D_MODEL
OFF_GATE = OFF_CF + 2 * D_MODEL
D_IN = OFF_GATE + 3 * D_MODEL

V7X_VMEM_BYTES = 64 * 1024 * 1024
VMEM_LIMIT_BYTES = V7X_VMEM_BYTES - 8 * 1024 * 1024
SUBLANES = 8
LANES = 128
HALO = 16
MOD_ROWS = 16
CONV_ROWS = 64
CONV_COLS = D_MODEL // N_KV_HEADS
SUB_ROWS = 256
Q_COLS = 256
BOUNDED_KEY_CHUNK = 2816
F32 = jnp.float32
BF16 = jnp.bfloat16


def _params(*sem, flags=None):
    return pltpu.CompilerParams(dimension_semantics=sem, vmem_limit_bytes=VMEM_LIMIT_BYTES,
                                flags=flags)


def _resident(shape):
    nd = len(shape)
    return pl.BlockSpec(shape, lambda *_: (0,) * nd, pipeline_mode=pl.Buffered(1))


def _mod_spec(mod):
    if mod.shape[0] > 1:
        return pl.BlockSpec((1, 1, 6 * D_MODEL), lambda b, *_: (b, 0, 0))
    return pl.BlockSpec((1, 1, 6 * D_MODEL), lambda *_: (0, 0, 0))


def _row_tile(seq, want):
    t = min(want, seq)
    assert seq % t == 0
    return t


def _mod_kernel(c_ref, w_ref, b_ref, o_ref):
    a = c_ref[...]
    a = a * jax.nn.sigmoid(a)
    o_ref[0] = jnp.dot(a, w_ref[0], preferred_element_type=F32,
                       precision=lax.Precision.HIGHEST) + b_ref[0]


def _modulation(cond, w_mod, b_mod):
    n_layers = w_mod.shape[0]
    tn = D_MODEL
    return pl.pallas_call(
        _mod_kernel,
        grid=(n_layers, 6 * D_MODEL // tn),
        in_specs=[
            pl.BlockSpec((MOD_ROWS, D_MODEL), lambda l, j: (0, 0)),
            pl.BlockSpec((1, D_MODEL, tn), lambda l, j: (l, 0, j)),
            pl.BlockSpec((1, 1, tn), lambda l, j: (l, 0, j)),
        ],
        out_specs=pl.BlockSpec((1, MOD_ROWS, tn), lambda l, j: (l, 0, j)),
        out_shape=jax.ShapeDtypeStruct((n_layers, MOD_ROWS, 6 * D_MODEL), F32),
        compiler_params=_params("arbitrary", "arbitrary"),
    )(cond, w_mod, b_mod[:, None, :])


def _sigmoid(x):
    return 0.5 * jnp.tanh(0.5 * x) + 0.5


def _modulated_rmsnorm(xt, g, shift, scale):
    ms = jnp.mean(xt * xt, axis=-1, keepdims=True)
    h = xt * lax.rsqrt(ms + NORM_EPS) * g
    return h * (1.0 + scale) + shift


def _head_norm_rope(ph, gain, cos, sin, rope):
    ms = jnp.mean(ph * ph, axis=-1, keepdims=True)
    y = ph * lax.rsqrt(ms + NORM_EPS) * gain
    if rope:
        lane = lax.broadcasted_iota(jnp.int32, y.shape, 1)
        first_half = (lane % (2 * ROPE_FREQS)) < ROPE_FREQS
        partner = jnp.where(first_half,
                            pltpu.roll(y, HEAD_DIM - ROPE_FREQS, 1),
                            pltpu.roll(y, ROPE_FREQS, 1))
        y = y * cos + partner * sin
    return y


def _inproj_kernel(x_ref, mod_ref, g_ref, w_ref, qg_ref, kg_ref, cos_ref, sin_ref, *outs,
                   rope, kv_only):
    tm = x_ref.shape[1]
    sub = min(SUB_ROWS, tm)
    for r0 in range(0, tm, sub):
        _inproj_rows(slice(r0, r0 + sub), x_ref, mod_ref, g_ref, w_ref, qg_ref, kg_ref,
                     cos_ref, sin_ref, outs, rope, kv_only)


def _inproj_rows(rows, x_ref, mod_ref, g_ref, w_ref, qg_ref, kg_ref, cos_ref, sin_ref, outs,
                 rope, kv_only):
    xt = x_ref[0, rows, :]
    h = _modulated_rmsnorm(xt, g_ref[...], mod_ref[0, :, 0:D_MODEL],
                           mod_ref[0, :, D_MODEL:2 * D_MODEL])
    hb = h.astype(BF16)
    base = OFF_K if kv_only else 0

    def proj(off, n):
        return jnp.dot(hb, w_ref[:, off - base:off - base + n], preferred_element_type=F32)

    cos = cos_ref[rows, :] if rope else None
    sin = sin_ref[rows, :] if rope else None
    if kv_only:
        k_out, vt_out = outs
    else:
        q_out, k_out, vt_out, bg_out, cg_out, u_out, gate_out = outs

    pk = proj(OFF_K, KV_WIDTH)
    for hd in range(N_KV_HEADS):
        sl = slice(hd * HEAD_DIM, (hd + 1) * HEAD_DIM)
        k_out[0, rows, sl] = _head_norm_rope(pk[:, sl], kg_ref[...], cos, sin, rope).astype(BF16)
    vt_out[0, :, rows] = proj(OFF_V, KV_WIDTH).T.astype(BF16)
    if kv_only:
        return

    pq = proj(OFF_Q, ATTN_WIDTH)
    for hd in range(N_Q_HEADS):
        sl = slice(hd * HEAD_DIM, (hd + 1) * HEAD_DIM)
        qh = _head_norm_rope(pq[:, sl], qg_ref[...], cos, sin, rope)
        q_out[0, hd, rows, :] = (qh * Q_SCALE_LOG2).astype(BF16)

    bg_out[0, rows, :] = proj(OFF_SC, D_MODEL).astype(BF16)
    cg_out[0, rows, :] = (proj(OFF_SC + D_MODEL, D_MODEL)
                          * proj(OFF_SC + 2 * D_MODEL, D_MODEL)).astype(BF16)
    cf_a = proj(OFF_CF, D_MODEL)
    u_out[0, rows, :] = (cf_a * _sigmoid(proj(OFF_CF + D_MODEL, D_MODEL))).astype(BF16)
    for j in range(3):
        sl = slice(j * D_MODEL, (j + 1) * D_MODEL)
        gate_out[0, rows, sl] = _sigmoid(proj(OFF_GATE + j * D_MODEL, D_MODEL)).astype(BF16)


def _inproj(x, mod, g_norm, w, q_gain, k_gain, cos, sin, *, rope, kv_only, tm):
    bsz, seq, _ = x.shape
    tm = _row_tile(seq, tm)
    tok = lambda width: pl.BlockSpec((1, tm, width), lambda b, i: (b, i, 0))
    tok_shape = lambda width: jax.ShapeDtypeStruct((bsz, seq, width), BF16)
    out_specs = [tok(KV_WIDTH), pl.BlockSpec((1, KV_WIDTH, tm), lambda b, i: (b, 0, i))]
    out_shape = [tok_shape(KV_WIDTH), jax.ShapeDtypeStruct((bsz, KV_WIDTH, seq), BF16)]
    if not kv_only:
        q_spec = pl.BlockSpec((1, N_Q_HEADS, tm, HEAD_DIM), lambda b, i: (b, 0, i, 0))
        q_shape = jax.ShapeDtypeStruct((bsz, N_Q_HEADS, seq, HEAD_DIM), BF16)
        out_specs = [q_spec] + out_specs + [tok(D_MODEL)] * 3 + [tok(3 * D_MODEL)]
        out_shape = [q_shape] + out_shape + [tok_shape(D_MODEL)] * 3 + [tok_shape(3 * D_MODEL)]
    return pl.pallas_call(
        functools.partial(_inproj_kernel, rope=rope, kv_only=kv_only),
        grid=(bsz, seq // tm),
        in_specs=[
            pl.BlockSpec((1, tm, D_MODEL), lambda b, i: (b, i, 0)),
            _mod_spec(mod),
            _resident((1, D_MODEL)),
            _resident(w.shape),
            _resident((1, HEAD_DIM)),
            _resident((1, HEAD_DIM)),
            pl.BlockSpec((tm, HEAD_DIM), lambda b, i: (i, 0)),
            pl.BlockSpec((tm, HEAD_DIM), lambda b, i: (i, 0)),
        ],
        out_specs=out_specs,
        out_shape=out_shape,
        compiler_params=_params("parallel", "parallel"),
    )(x, mod, g_norm, w, q_gain, k_gain, cos, sin)


def _fill_ext(ext, prev_ref, main_ref, next_ref, first, last):
    rows = main_ref.shape[1]
    ext[0:HALO, :] = jnp.where(first, 0.0, prev_ref[0].astype(F32))
    ext[HALO:HALO + rows, :] = main_ref[0].astype(F32)
    ext[HALO + rows:2 * HALO + rows, :] = jnp.where(last, 0.0, next_ref[0].astype(F32))


def _dwconv(ext, shifted, wb_ref, taps, out_ref, zeros):
    rows, cols = out_ref.shape[1], out_ref.shape[2]
    offs = [HALO + k - taps // 2 for k in range(taps)]
    phases = sorted({o % SUBLANES for o in offs} - {0})
    n = rows + 2 * HALO - SUBLANES
    for i, r in enumerate(phases):
        shifted[i, 0:n, :] = ext[r:r + n, :]
    step = min(CONV_ROWS, rows)
    groups = step // SUBLANES
    slabs = [(r0, l0) for r0 in range(0, rows, step) for l0 in range(0, cols, LANES)]
    for s, (r0, l0) in enumerate(slabs):
        lanes = slice(l0, l0 + LANES)
        acc = jnp.broadcast_to(zeros[s * len(zeros) // len(slabs)][None],
                               (groups, SUBLANES, LANES))
        for k, o in enumerate(offs):
            r = o % SUBLANES
            src = ext if r == 0 else shifted.at[phases.index(r)]
            slab = src[r0 + o - r:r0 + o - r + step, lanes]
            acc = acc + slab.reshape(groups, SUBLANES, LANES) * wb_ref[k, :, lanes][None]
        out_ref[0, r0:r0 + step, lanes] = acc.reshape(step, LANES).astype(out_ref.dtype)


def _conv_side_job(cg_refs, u_refs, wscc_ref, wcfc_ref, cc_ref, cu_ref, scratch, zeros=None):
    ext_c, sh_c, ext_u, sh_u = scratch
    if zeros is None:
        zeros = [jnp.zeros((SUBLANES, LANES), F32)]
    i = pl.program_id(2)
    first = i == 0
    last = i == pl.num_programs(2) - 1
    _fill_ext(ext_c, cg_refs[1], cg_refs[0], cg_refs[2], first, last)
    _dwconv(ext_c, sh_c, wscc_ref, SC_KERNEL, cc_ref, zeros[:1])
    _fill_ext(ext_u, u_refs[1], u_refs[0], u_refs[2], first, last)
    _dwconv(ext_u, sh_u, wcfc_ref, CF_KERNEL, cu_ref, zeros)


def _scores_t(k_rows, qh):
    return lax.dot_general(k_rows, qh, (((1,), (1,)), ((), ())), preferred_element_type=F32)


def _attn_bounded_kernel(q_ref, k_ref, vt_ref, cg_ref, cgp_ref, cgn_ref, u_ref, up_ref, un_ref,
                         wscc_ref, wcfc_ref, o_ref, cc_ref, cu_ref, *scratch, kc):
    n_keys = k_ref.shape[1]
    tq = min(Q_COLS, q_ref.shape[2])
    zeros = []
    for t0, h in [(t0, h) for t0 in range(0, q_ref.shape[2], tq) for h in range(Q_PER_KV)]:
        qh = q_ref[0, h, t0:t0 + tq, :]
        acc = jnp.zeros((HEAD_DIM, tq), F32)
        lsum = jnp.zeros((SUBLANES, tq), F32)
        for k0, k1 in _key_chunks(n_keys, kc):
            p = jnp.exp2(_scores_t(k_ref[0, k0:k1, :], qh))
            zeros.append(jnp.minimum(p[0:SUBLANES, 0:LANES], 0.0))
            lsum = lsum + jnp.sum(p.reshape((k1 - k0) // SUBLANES, SUBLANES, tq), axis=0)
            acc = acc + jnp.dot(vt_ref[0, :, k0:k1], p.astype(BF16),
                                preferred_element_type=F32)
        inv_l = 1.0 / jnp.sum(lsum, axis=0, keepdims=True)
        o_ref[0, h, t0:t0 + tq, :] = (acc * inv_l).T.astype(BF16)
    _conv_side_job((cg_ref, cgp_ref, cgn_ref), (u_ref, up_ref, un_ref), wscc_ref, wcfc_ref,
                   cc_ref, cu_ref, scratch, zeros)


def _attn_online_kernel(q_ref, k_ref, vt_ref, cg_ref, cgp_ref, cgn_ref, u_ref, up_ref, un_ref,
                        wscc_ref, wcfc_ref, o_ref, cc_ref, cu_ref, *scratch, kc):
    n_keys = k_ref.shape[1]
    tq = min(Q_COLS, q_ref.shape[2])

    def head(i, _):
        h = i % Q_PER_KV
        t0 = pl.multiple_of((i // Q_PER_KV) * tq, tq)
        qh = q_ref[0, h, pl.ds(t0, tq), :]

        def chunk(c, carry):
            m, l, acc = carry
            k0 = pl.multiple_of(c * kc, kc)
            st = _scores_t(k_ref[0, pl.ds(k0, kc), :], qh)
            m_new = jnp.maximum(m, jnp.max(st, axis=0, keepdims=True))
            alpha = jnp.exp2(m - m_new)
            p = jnp.exp2(st - m_new)
            l = alpha * l + jnp.sum(p, axis=0, keepdims=True)
            acc = acc * alpha + jnp.dot(vt_ref[0, :, pl.ds(k0, kc)], p.astype(BF16),
                                        preferred_element_type=F32)
            return m_new, l, acc

        init = (jnp.full((1, tq), -jnp.inf, F32), jnp.zeros((1, tq), F32),
                jnp.zeros((HEAD_DIM, tq), F32))
        _, l, acc = lax.fori_loop(0, n_keys // kc, chunk, init)
        o_ref[0, h, pl.ds(t0, tq), :] = (acc / l).T.astype(BF16)
        return 0

    lax.fori_loop(0, Q_PER_KV * (q_ref.shape[2] // tq), head, 0)
    _conv_side_job((cg_ref, cgp_ref, cgn_ref), (u_ref, up_ref, un_ref), wscc_ref, wcfc_ref,
                   cc_ref, cu_ref, scratch)


def _key_chunk(n_keys):
    for kc in (768, 1024, 512, 640, 256, 128):
        if n_keys % kc == 0:
            return kc
    raise ValueError(f"unsupported key count {n_keys}")


def _key_chunks(n_keys, kc):
    rem = n_keys % kc
    return ([(0, rem)] if rem else []) + [(k0, k0 + kc) for k0 in range(rem, n_keys, kc)]


def _attention(q, k, vt, cg, u, wscc, wcfc, *, tq, bounded):
    bsz, _, seq, _ = q.shape
    n_keys = k.shape[1]
    tq = _row_tile(seq, tq)
    hb = tq // HALO
    n_halo = seq // HALO
    body = _attn_bounded_kernel if bounded else _attn_online_kernel
    qo_spec = pl.BlockSpec((1, Q_PER_KV, tq, HEAD_DIM), lambda b, g, i: (b, g, i, 0))
    main = pl.BlockSpec((1, tq, CONV_COLS), lambda b, g, i: (b, i, g))
    prev = pl.BlockSpec((1, HALO, CONV_COLS), lambda b, g, i: (b, jnp.maximum(i * hb - 1, 0), g))
    nxt = pl.BlockSpec((1, HALO, CONV_COLS),
                       lambda b, g, i: (b, jnp.minimum((i + 1) * hb, n_halo - 1), g))
    taps = lambda n: pl.BlockSpec((n, SUBLANES, CONV_COLS), lambda b, g, i: (0, 0, g))
    ext_rows = tq + 2 * HALO
    tok_shape = jax.ShapeDtypeStruct((bsz, seq, D_MODEL), BF16)
    return pl.pallas_call(
        functools.partial(body, kc=BOUNDED_KEY_CHUNK if bounded else _key_chunk(n_keys)),
        grid=(bsz, N_KV_HEADS, seq // tq),
        in_specs=[
            qo_spec,
            pl.BlockSpec((1, n_keys, HEAD_DIM), lambda b, g, i: (b, 0, g)),
            pl.BlockSpec((1, HEAD_DIM, n_keys), lambda b, g, i: (b, g, 0)),
            main, prev, nxt,
            main, prev, nxt,
            taps(SC_KERNEL), taps(CF_KERNEL),
        ],
        out_specs=[qo_spec, main, main],
        out_shape=[jax.ShapeDtypeStruct((bsz, N_Q_HEADS, seq, HEAD_DIM), BF16), tok_shape, tok_shape],
        scratch_shapes=[pltpu.VMEM((ext_rows, CONV_COLS), F32),
                        pltpu.VMEM((2, ext_rows, CONV_COLS), F32),
                        pltpu.VMEM((ext_rows, CONV_COLS), F32),
                        pltpu.VMEM((SUBLANES - 1, ext_rows, CONV_COLS), F32)],
        compiler_params=_params("parallel", "parallel", "arbitrary"),
    )(q, k, vt, cg, cg, cg, u, u, u, wscc, wcfc)


def _attention_any(scores_bounded, *operands, tq):
    return lax.cond(scores_bounded,
                    functools.partial(_attention, tq=tq, bounded=True),
                    functools.partial(_attention, tq=tq, bounded=False),
                    *operands)


def _mix_kernel(x_ref, attn_ref, bg_ref, cc_ref, cu_ref, gate_ref, mod_ref,
                wao_ref, wsc_ref, wcf_ref, wo_ref, bcfc_ref, gln_ref, bln_ref, bcfo_ref, o_ref):
    dot = functools.partial(jnp.dot, preferred_element_type=F32)
    tm = x_ref.shape[1]
    sub = min(SUB_ROWS, tm)
    for r0 in range(0, tm, sub):
        rows = slice(r0, r0 + sub)
        attn = jnp.concatenate([attn_ref[0, hd, rows, :] for hd in range(N_Q_HEADS)], axis=-1)
        y_attn = dot(attn, wao_ref[...])
        y_sc = dot(bg_ref[0, rows, :] * cc_ref[0, rows, :], wsc_ref[...])

        uc = cu_ref[0, rows, :].astype(F32) + bcfc_ref[...]
        mu = jnp.mean(uc, axis=-1, keepdims=True)
        xc = uc - mu
        un = xc * lax.rsqrt(jnp.mean(xc * xc, axis=-1, keepdims=True) + LN_EPS)
        un = un * gln_ref[...] + bln_ref[...]
        un = un * _sigmoid(un)
        y_cf = dot(un.astype(BF16), wcf_ref[...]) + bcfo_ref[...]

        ga = gate_ref[0, rows, 0:D_MODEL].astype(F32)
        gb = gate_ref[0, rows, D_MODEL:2 * D_MODEL].astype(F32)
        gc = gate_ref[0, rows, 2 * D_MODEL:3 * D_MODEL].astype(F32)
        merged = ga * y_attn + gb * y_sc + gc * y_cf
        y = dot(merged.astype(BF16), wo_ref[...])
        gt1 = mod_ref[0, :, 2 * D_MODEL:3 * D_MODEL]
        o_ref[0, rows, :] = x_ref[0, rows, :] + gt1 * y


def _mixer(x, attn, bg, cc, cu, gates, mod, lw, *, tm):
    bsz, seq, _ = x.shape
    tm = _row_tile(seq, tm)
    tok = lambda width: pl.BlockSpec((1, tm, width), lambda b, i: (b, i, 0))
    row = _resident((1, D_MODEL))
    sq = _resident((D_MODEL, D_MODEL))
    return pl.pallas_call(
        _mix_kernel,
        grid=(bsz, seq // tm),
        in_specs=[
            tok(D_MODEL),
            pl.BlockSpec((1, N_Q_HEADS, tm, HEAD_DIM), lambda b, i: (b, 0, i, 0)),
            tok(D_MODEL), tok(D_MODEL), tok(D_MODEL), tok(3 * D_MODEL),
            _mod_spec(mod),
            sq, sq, sq, sq,
            row, row, row, row,
        ],
        out_specs=tok(D_MODEL),
        out_shape=jax.ShapeDtypeStruct((bsz, seq, D_MODEL), F32),
        compiler_params=_params("parallel", "parallel"),
    )(x, attn, bg, cc, cu, gates, mod,
      lw["w_attn_out"], lw["w_sc_out"], lw["w_cf_out"], lw["w_o"],
      lw["b_cf_conv"], lw["g_cf_ln"], lw["b_cf_ln"], lw["b_cf_out"])


def _mlp_kernel(x_ref, mod_ref, g_ref, w1_ref, w2_ref, gf_ref, o_ref, *, final_norm):
    tm = x_ref.shape[1]
    sub = min(SUB_ROWS, tm)
    for r0 in range(0, tm, sub):
        rows = slice(r0, r0 + sub)
        xt = x_ref[0, rows, :]
        h = _modulated_rmsnorm(xt, g_ref[...], mod_ref[0, :, 3 * D_MODEL:4 * D_MODEL],
                               mod_ref[0, :, 4 * D_MODEL:5 * D_MODEL])
        hb = h.astype(BF16)
        acc = jnp.zeros(xt.shape, F32)
        for j in range(D_FF // D_MODEL):
            sl = slice(j * D_MODEL, (j + 1) * D_MODEL)
            hid = jnp.dot(hb, w1_ref[:, sl], preferred_element_type=F32)
            hid = jnp.square(jnp.maximum(hid, 0.0)).astype(BF16)
            acc = acc + jnp.dot(hid, w2_ref[sl, :], preferred_element_type=F32)
        y = xt + mod_ref[0, :, 5 * D_MODEL:6 * D_MODEL] * acc
        if final_norm:
            ms = jnp.mean(y * y, axis=-1, keepdims=True)
            y = y * lax.rsqrt(ms + NORM_EPS) * gf_ref[...]
        o_ref[0, rows, :] = y


def _mlp(x, mod, g_norm, w1, w2, g_final, *, final_norm, tm):
    bsz, seq, _ = x.shape
    tm = _row_tile(seq, tm)
    tok = pl.BlockSpec((1, tm, D_MODEL), lambda b, i: (b, i, 0))
    return pl.pallas_call(
        functools.partial(_mlp_kernel, final_norm=final_norm),
        grid=(bsz, seq // tm),
        in_specs=[
            tok,
            _mod_spec(mod),
            _resident((1, D_MODEL)),
            _resident((D_MODEL, D_FF)),
            _resident((D_FF, D_MODEL)),
            _resident((1, D_MODEL)),
        ],
        out_specs=tok,
        out_shape=jax.ShapeDtypeStruct((bsz, seq, D_MODEL), F32),
        compiler_params=_params("parallel", "parallel"),
    )(x, mod, g_norm, w1, w2, g_final)


def _rope_tables(n_tokens):
    rows = n_tokens // GRID_W
    row = jnp.repeat(jnp.arange(rows), GRID_W)
    col = jnp.tile(jnp.arange(GRID_W), rows)
    pos = jnp.stack([row, col], axis=-1).astype(F32)
    inv_freq = ROPE_THETA ** (-jnp.arange(ROPE_FREQS, dtype=F32) * 2.0 / ROPE_AXIS_DIM)
    ang = pos[:, :, None] * inv_freq
    cos, sin = jnp.cos(ang), jnp.sin(ang)
    cos_t = jnp.concatenate([cos, cos], axis=-1).reshape(n_tokens, HEAD_DIM)
    sin_t = jnp.concatenate([-sin, sin], axis=-1).reshape(n_tokens, HEAD_DIM)
    return cos_t, sin_t


def kernel(x, c, ctx, c_ctx, w_mod, b_mod, g_norm1, g_norm2, w_in, q_gain, k_gain, w_attn_out,
           w_sc_conv, w_sc_out, w_cf_conv, b_cf_conv, g_cf_ln, b_cf_ln, w_cf_out, b_cf_out,
           w_o, w_mlp_in, w_mlp_out, g_final):
    bsz, seq, _ = x.shape
    n_ctx = ctx.shape[1]
    depth = w_mod.shape[0]
    assert bsz + 1 <= MOD_ROWS and seq % GRID_W == 0
    row = lambda v: v.reshape(1, -1)
    taps = lambda w: jnp.broadcast_to(w[:, None, :], (w.shape[0], SUBLANES, D_MODEL))

    cond = jnp.zeros((MOD_ROWS, D_MODEL), F32).at[:bsz].set(c).at[bsz].set(c_ctx)
    mod_all = _modulation(cond, w_mod, b_mod)
    cos_lat, sin_lat = _rope_tables(seq)
    cos_ctx, sin_ctx = cos_lat[:n_ctx], sin_lat[:n_ctx]

    for l in range(depth):
        last = l == depth - 1
        mod_lat = mod_all[l, :bsz][:, None, :]
        mod_ctx = mod_all[l, bsz:bsz + 1][:, None, :]
        w_in_l = w_in[l].astype(BF16)
        lw = dict(
            w_attn_out=w_attn_out[l].astype(BF16), w_sc_out=w_sc_out[l].astype(BF16),
            w_cf_out=w_cf_out[l].astype(BF16), w_o=w_o[l].astype(BF16),
            b_cf_conv=row(b_cf_conv[l]), g_cf_ln=row(g_cf_ln[l]), b_cf_ln=row(b_cf_ln[l]),
            b_cf_out=row(b_cf_out[l]))
        conv_w = (taps(w_sc_conv[l]), taps(w_cf_conv[l]))
        w1 = w_mlp_in[l].astype(BF16)
        w2 = w_mlp_out[l].astype(BF16)
        gains = (row(q_gain[l]), row(k_gain[l]))
        score_bound = HEAD_DIM ** 0.5 * jnp.max(jnp.abs(q_gain[l])) * jnp.max(jnp.abs(k_gain[l]))
        bounded = score_bound <= SAFE_SCORE_BOUND

        if last:
            k_ctx, vt_ctx = _inproj(ctx, mod_ctx, row(g_norm1[l]), w_in_l[:, OFF_K:OFF_SC], *gains,
                                    cos_ctx, sin_ctx, rope=False, kv_only=True, tm=256)
        else:
            q_c, k_ctx, vt_ctx, bg_c, cg_c, u_c, gates_c = _inproj(
                ctx, mod_ctx, row(g_norm1[l]), w_in_l, *gains, cos_ctx, sin_ctx,
                rope=False, kv_only=False, tm=256)

        q, k_lat, vt_lat, bg, cg, u, gates = _inproj(
            x, mod_lat, row(g_norm1[l]), w_in_l, *gains, cos_lat, sin_lat,
            rope=True, kv_only=False, tm=512)
        k_all = jnp.concatenate([k_ctx, k_lat], axis=1)
        vt_all = jnp.concatenate([vt_ctx, vt_lat], axis=2)
        attn, cc, cu = _attention_any(bounded, q, k_all, vt_all, cg, u, *conv_w, tq=512)
        x = _mixer(x, attn, bg, cc, cu, gates, mod_lat, lw, tm=512)
        x = _mlp(x, mod_lat, row(g_norm2[l]), w1, w2, row(g_final), final_norm=last, tm=1024)

        if not last:
            attn_c, cc_c, cu_c = _attention_any(bounded, q_c, k_ctx, vt_ctx, cg_c, u_c, *conv_w,
                                                tq=256)
            ctx = _mixer(ctx, attn_c, bg_c, cc_c, cu_c, gates_c, mod_ctx, lw, tm=256)
            ctx = _mlp(ctx, mod_ctx, row(g_norm2[l]), w1, w2, row(g_final), final_norm=False, tm=256)
    return x
```

```python
import functools

import jax
import jax.numpy as jnp
from jax import lax
from jax.experimental import pallas as pl
from jax.experimental.pallas import tpu as pltpu

D_MODEL = 1024
GRID_W = 64
HEAD_DIM = 128
N_Q_HEADS = 8
N_KV_HEADS = 2
Q_PER_KV = N_Q_HEADS // N_KV_HEADS
ATTN_WIDTH = N_Q_HEADS * HEAD_DIM
KV_WIDTH = N_KV_HEADS * HEAD_DIM
ROPE_THETA = 10000.0
ROPE_AXIS_DIM = HEAD_DIM // 2
ROPE_FREQS = ROPE_AXIS_DIM // 2
SC_KERNEL = 3
CF_KERNEL = 31
D_FF = 4 * D_MODEL
NORM_EPS = 1e-6
LN_EPS = 1e-5
LOG2E = 1.4426950408889634
Q_SCALE_LOG2 = HEAD_DIM ** -0.5 * LOG2E
SAFE_SCORE_BOUND = 60.0
OFF_Q = 0
OFF_K = OFF_Q + ATTN_WIDTH
OFF_V = OFF_K + KV_WIDTH
OFF_SC = OFF_V + KV_WIDTH
OFF_CF = OFF_SC + 3 * D_MODEL
OFF_GATE = OFF_CF + 2 * D_MODEL
D_IN = OFF_GATE + 3 * D_MODEL

V7X_VMEM_BYTES = 64 * 1024 * 1024
VMEM_LIMIT_BYTES = V7X_VMEM_BYTES - 8 * 1024 * 1024
SUBLANES = 8
LANES = 128
HALO = 16
MOD_ROWS = 16
CONV_ROWS = 64
CONV_COLS = D_MODEL // N_KV_HEADS
SUB_ROWS = 256
Q_COLS = 256
BOUNDED_KEY_CHUNK = 2816
F32 = jnp.float32
BF16 = jnp.bfloat16


def _params(*sem, flags=None):
    return pltpu.CompilerParams(dimension_semantics=sem, vmem_limit_bytes=VMEM_LIMIT_BYTES,
                                flags=flags)


def _resident(shape):
    nd = len(shape)
    return pl.BlockSpec(shape, lambda *_: (0,) * nd, pipeline_mode=pl.Buffered(1))


def _mod_spec(mod):
    if mod.shape[0] > 1:
        return pl.BlockSpec((1, 1, 6 * D_MODEL), lambda b, *_: (b, 0, 0))
    return pl.BlockSpec((1, 1, 6 * D_MODEL), lambda *_: (0, 0, 0))


def _row_tile(seq, want):
    t = min(want, seq)
    assert seq % t == 0
    return t


def _mod_kernel(c_ref, w_ref, b_ref, o_ref):
    a = c_ref[...]
    a = a * jax.nn.sigmoid(a)
    o_ref[0] = jnp.dot(a, w_ref[0], preferred_element_type=F32,
                       precision=lax.Precision.HIGHEST) + b_ref[0]


def _modulation(cond, w_mod, b_mod):
    n_layers = w_mod.shape[0]
    tn = D_MODEL
    return pl.pallas_call(
        _mod_kernel,
        grid=(n_layers, 6 * D_MODEL // tn),
        in_specs=[
            pl.BlockSpec((MOD_ROWS, D_MODEL), lambda l, j: (0, 0)),
            pl.BlockSpec((1, D_MODEL, tn), lambda l, j: (l, 0, j)),
            pl.BlockSpec((1, 1, tn), lambda l, j: (l, 0, j)),
        ],
        out_specs=pl.BlockSpec((1, MOD_ROWS, tn), lambda l, j: (l, 0, j)),
        out_shape=jax.ShapeDtypeStruct((n_layers, MOD_ROWS, 6 * D_MODEL), F32),
        compiler_params=_params("arbitrary", "arbitrary"),
    )(cond, w_mod, b_mod[:, None, :])


def _sigmoid(x):
    return 0.5 * jnp.tanh(0.5 * x) + 0.5


def _modulated_rmsnorm(xt, g, shift, scale):
    ms = jnp.mean(xt * xt, axis=-1, keepdims=True)
    h = xt * lax.rsqrt(ms + NORM_EPS) * g
    return h * (1.0 + scale) + shift


def _head_norm_rope(ph, gain, cos, sin, rope):
    ms = jnp.mean(ph * ph, axis=-1, keepdims=True)
    y = ph * lax.rsqrt(ms + NORM_EPS) * gain
    if rope:
        lane = lax.broadcasted_iota(jnp.int32, y.shape, 1)
        first_half = (lane % (2 * ROPE_FREQS)) < ROPE_FREQS
        partner = jnp.where(first_half,
                            pltpu.roll(y, HEAD_DIM - ROPE_FREQS, 1),
                            pltpu.roll(y, ROPE_FREQS, 1))
        y = y * cos + partner * sin
    return y


def _inproj_kernel(x_ref, mod_ref, g_ref, w_ref, qg_ref, kg_ref, cos_ref, sin_ref, *outs,
                   rope, kv_only):
    tm = x_ref.shape[1]
    sub = min(SUB_ROWS, tm)
    for r0 in range(0, tm, sub):
        _inproj_rows(slice(r0, r0 + sub), x_ref, mod_ref, g_ref, w_ref, qg_ref, kg_ref,
                     cos_ref, sin_ref, outs, rope, kv_only)


def _inproj_rows(rows, x_ref, mod_ref, g_ref, w_ref, qg_ref, kg_ref, cos_ref, sin_ref, outs,
                 rope, kv_only):
    xt = x_ref[0, rows, :]
    h = _modulated_rmsnorm(xt, g_ref[...], mod_ref[0, :, 0:D_MODEL],
                           mod_ref[0, :, D_MODEL:2 * D_MODEL])
    hb = h.astype(BF16)
    base = OFF_K if kv_only else 0

    def proj(off, n):
        return jnp.dot(hb, w_ref[:, off - base:off - base + n], preferred_element_type=F32)

    cos = cos_ref[rows, :] if rope else None
    sin = sin_ref[rows, :] if rope else None
    if kv_only:
        k_out, vt_out = outs
    else:
        q_out, k_out, vt_out, bg_out, cg_out, u_out, gate_out = outs

    pk = proj(OFF_K, KV_WIDTH)
    for hd in range(N_KV_HEADS):
        sl = slice(hd * HEAD_DIM, (hd + 1) * HEAD_DIM)
        k_out[0, rows, sl] = _head_norm_rope(pk[:, sl], kg_ref[...], cos, sin, rope).astype(BF16)
    vt_out[0, :, rows] = proj(OFF_V, KV_WIDTH).T.astype(BF16)
    if kv_only:
        return

    pq = proj(OFF_Q, ATTN_WIDTH)
    for hd in range(N_Q_HEADS):
        sl = slice(hd * HEAD_DIM, (hd + 1) * HEAD_DIM)
        qh = _head_norm_rope(pq[:, sl], qg_ref[...], cos, sin, rope)
        q_out[0, hd, rows, :] = (qh * Q_SCALE_LOG2).astype(BF16)

    bg_out[0, rows, :] = proj(OFF_SC, D_MODEL).astype(BF16)
    cg_out[0, rows, :] = (proj(OFF_SC + D_MODEL, D_MODEL)
                          * proj(OFF_SC + 2 * D_MODEL, D_MODEL)).astype(BF16)
    cf_a = proj(OFF_CF, D_MODEL)
    u_out[0, rows, :] = (cf_a * _sigmoid(proj(OFF_CF + D_MODEL, D_MODEL))).astype(BF16)
    for j in range(3):
        sl = slice(j * D_MODEL, (j + 1) * D_MODEL)
        gate_out[0, rows, sl] = _sigmoid(proj(OFF_GATE + j * D_MODEL, D_MODEL)).astype(BF16)


def _inproj(x, mod, g_norm, w, q_gain, k_gain, cos, sin, *, rope, kv_only, tm):
    bsz, seq, _ = x.shape
    tm = _row_tile(seq, tm)
    tok = lambda width: pl.BlockSpec((1, tm, width), lambda b, i: (b, i, 0))
    tok_shape = lambda width: jax.ShapeDtypeStruct((bsz, seq, width), BF16)
    out_specs = [tok(KV_WIDTH), pl.BlockSpec((1, KV_WIDTH, tm), lambda b, i: (b, 0, i))]
    out_shape = [tok_shape(KV_WIDTH), jax.ShapeDtypeStruct((bsz, KV_WIDTH, seq), BF16)]
    if not kv_only:
        q_spec = pl.BlockSpec((1, N_Q_HEADS, tm, HEAD_DIM), lambda b, i: (b, 0, i, 0))
        q_shape = jax.ShapeDtypeStruct((bsz, N_Q_HEADS, seq, HEAD_DIM), BF16)
        out_specs = [q_spec] + out_specs + [tok(D_MODEL)] * 3 + [tok(3 * D_MODEL)]
        out_shape = [q_shape] + out_shape + [tok_shape(D_MODEL)] * 3 + [tok_shape(3 * D_MODEL)]
    return pl.pallas_call(
        functools.partial(_inproj_kernel, rope=rope, kv_only=kv_only),
        grid=(bsz, seq // tm),
        in_specs=[
            pl.BlockSpec((1, tm, D_MODEL), lambda b, i: (b, i, 0)),
            _mod_spec(mod),
            _resident((1, D_MODEL)),
            _resident(w.shape),
            _resident((1, HEAD_DIM)),
            _resident((1, HEAD_DIM)),
            pl.BlockSpec((tm, HEAD_DIM), lambda b, i: (i, 0)),
            pl.BlockSpec((tm, HEAD_DIM), lambda b, i: (i, 0)),
        ],
        out_specs=out_specs,
        out_shape=out_shape,
        compiler_params=_params("parallel", "parallel"),
    )(x, mod, g_norm, w, q_gain, k_gain, cos, sin)


def _fill_ext(ext, prev_ref, main_ref, next_ref, first, last):
    rows = main_ref.shape[1]
    ext[0:HALO, :] = jnp.where(first, 0.0, prev_ref[0].astype(F32))
    ext[HALO:HALO + rows, :] = main_ref[0].astype(F32)
    ext[HALO + rows:2 * HALO + rows, :] = jnp.where(last, 0.0, next_ref[0].astype(F32))


def _dwconv(ext, wb_ref, taps, out_ref, zeros):
    rows, cols = out_ref.shape[1], out_ref.shape[2]
    offs = [HALO + k - taps // 2 for k in range(taps)]
    step = min(CONV_ROWS, rows)
    groups = step // SUBLANES
    slabs = [(r0, l0) for r0 in range(0, rows, step) for l0 in range(0, cols, LANES)]
    for s, (r0, l0) in enumerate(slabs):
        lanes = slice(l0, l0 + LANES)
        acc = jnp.broadcast_to(zeros[s * len(zeros) // len(slabs)][None],
                               (groups, SUBLANES, LANES))
        for r in sorted({o % SUBLANES for o in offs}):
            aligned = [(k, o - r) for k, o in enumerate(offs) if o % SUBLANES == r]
            span = max(a for _, a in aligned) + step
            window = ext[r0 + r:r0 + r + span, lanes]
            for k, a in aligned:
                slab = window[a:a + step].reshape(groups, SUBLANES, LANES)
                acc = acc + slab * wb_ref[k, :, lanes][None]
        out_ref[0, r0:r0 + step, lanes] = acc.reshape(step, LANES).astype(out_ref.dtype)


def _conv_side_job(cg_refs, u_refs, wscc_ref, wcfc_ref, cc_ref, cu_ref, scratch, zeros=None):
    ext_c, ext_u = scratch
    if zeros is None:
        zeros = [jnp.zeros((SUBLANES, LANES), F32)]
    i = pl.program_id(2)
    first = i == 0
    last = i == pl.num_programs(2) - 1
    _fill_ext(ext_c, cg_refs[1], cg_refs[0], cg_refs[2], first, last)
    _dwconv(ext_c, wscc_ref, SC_KERNEL, cc_ref, zeros[:1])
    _fill_ext(ext_u, u_refs[1], u_refs[0], u_refs[2], first, last)
    _dwconv(ext_u, wcfc_ref, CF_KERNEL, cu_ref, zeros)


def _scores_t(k_rows, qh):
    return lax.dot_general(k_rows, qh, (((1,), (1,)), ((), ())), preferred_element_type=F32)


def _attn_bounded_kernel(q_ref, k_ref, vt_ref, cg_ref, cgp_ref, cgn_ref, u_ref, up_ref, un_ref,
                         wscc_ref, wcfc_ref, o_ref, cc_ref, cu_ref, *scratch, kc):
    n_keys = k_ref.shape[1]
    tq = min(Q_COLS, q_ref.shape[2])
    zeros = []
    for t0, h in [(t0, h) for t0 in range(0, q_ref.shape[2], tq) for h in range(Q_PER_KV)]:
        qh = q_ref[0, h, t0:t0 + tq, :]
        acc = jnp.zeros((HEAD_DIM, tq), F32)
        lsum = jnp.zeros((SUBLANES, tq), F32)
        for k0, k1 in _key_chunks(n_keys, kc):
            p = jnp.exp2(_scores_t(k_ref[0, k0:k1, :], qh))
            zeros.append(jnp.minimum(p[0:SUBLANES, 0:LANES], 0.0))
            lsum = lsum + jnp.sum(p.reshape((k1 - k0) // SUBLANES, SUBLANES, tq), axis=0)
            acc = acc + jnp.dot(vt_ref[0, :, k0:k1], p.astype(BF16),
                                preferred_element_type=F32)
        inv_l = 1.0 / jnp.sum(lsum, axis=0, keepdims=True)
        o_ref[0, h, t0:t0 + tq, :] = (acc * inv_l).T.astype(BF16)
    _conv_side_job((cg_ref, cgp_ref, cgn_ref), (u_ref, up_ref, un_ref), wscc_ref, wcfc_ref,
                   cc_ref, cu_ref, scratch, zeros)


def _attn_online_kernel(q_ref, k_ref, vt_ref, cg_ref, cgp_ref, cgn_ref, u_ref, up_ref, un_ref,
                        wscc_ref, wcfc_ref, o_ref, cc_ref, cu_ref, *scratch, kc):
    n_keys = k_ref.shape[1]
    tq = min(Q_COLS, q_ref.shape[2])

    def head(i, _):
        h = i % Q_PER_KV
        t0 = pl.multiple_of((i // Q_PER_KV) * tq, tq)
        qh = q_ref[0, h, pl.ds(t0, tq), :]

        def chunk(c, carry):
            m, l, acc = carry
            k0 = pl.multiple_of(c * kc, kc)
            st = _scores_t(k_ref[0, pl.ds(k0, kc), :], qh)
            m_new = jnp.maximum(m, jnp.max(st, axis=0, keepdims=True))
            alpha = jnp.exp2(m - m_new)
            p = jnp.exp2(st - m_new)
            l = alpha * l + jnp.sum(p, axis=0, keepdims=True)
            acc = acc * alpha + jnp.dot(vt_ref[0, :, pl.ds(k0, kc)], p.astype(BF16),
                                        preferred_element_type=F32)
            return m_new, l, acc

        init = (jnp.full((1, tq), -jnp.inf, F32), jnp.zeros((1, tq), F32),
                jnp.zeros((HEAD_DIM, tq), F32))
        _, l, acc = lax.fori_loop(0, n_keys // kc, chunk, init)
        o_ref[0, h, pl.ds(t0, tq), :] = (acc / l).T.astype(BF16)
        return 0

    lax.fori_loop(0, Q_PER_KV * (q_ref.shape[2] // tq), head, 0)
    _conv_side_job((cg_ref, cgp_ref, cgn_ref), (u_ref, up_ref, un_ref), wscc_ref, wcfc_ref,
                   cc_ref, cu_ref, scratch)


def _key_chunk(n_keys):
    for kc in (768, 1024, 512, 640, 256, 128):
        if n_keys % kc == 0:
            return kc
    raise ValueError(f"unsupported key count {n_keys}")


def _key_chunks(n_keys, kc):
    rem = n_keys % kc
    return ([(0, rem)] if rem else []) + [(k0, k0 + kc) for k0 in range(rem, n_keys, kc)]


def _attention(q, k, vt, cg, u, wscc, wcfc, *, tq, bounded):
    bsz, _, seq, _ = q.shape
    n_keys = k.shape[1]
    tq = _row_tile(seq, tq)
    hb = tq // HALO
    n_halo = seq // HALO
    body = _attn_bounded_kernel if bounded else _attn_online_kernel
    qo_spec = pl.BlockSpec((1, Q_PER_KV, tq, HEAD_DIM), lambda b, g, i: (b, g, i, 0))
    main = pl.BlockSpec((1, tq, CONV_COLS), lambda b, g, i: (b, i, g))
    prev = pl.BlockSpec((1, HALO, CONV_COLS), lambda b, g, i: (b, jnp.maximum(i * hb - 1, 0), g))
    nxt = pl.BlockSpec((1, HALO, CONV_COLS),
                       lambda b, g, i: (b, jnp.minimum((i + 1) * hb, n_halo - 1), g))
    taps = lambda n: pl.BlockSpec((n, SUBLANES, CONV_COLS), lambda b, g, i: (0, 0, g))
    ext_rows = tq + 2 * HALO
    tok_shape = jax.ShapeDtypeStruct((bsz, seq, D_MODEL), BF16)
    return pl.pallas_call(
        functools.partial(body, kc=BOUNDED_KEY_CHUNK if bounded else _key_chunk(n_keys)),
        grid=(bsz, N_KV_HEADS, seq // tq),
        in_specs=[
            qo_spec,
            pl.BlockSpec((1, n_keys, HEAD_DIM), lambda b, g, i: (b, 0, g)),
            pl.BlockSpec((1, HEAD_DIM, n_keys), lambda b, g, i: (b, g, 0)),
            main, prev, nxt,
            main, prev, nxt,
            taps(SC_KERNEL), taps(CF_KERNEL),
        ],
        out_specs=[qo_spec, main, main],
        out_shape=[jax.ShapeDtypeStruct((bsz, N_Q_HEADS, seq, HEAD_DIM), BF16), tok_shape, tok_shape],
        scratch_shapes=[pltpu.VMEM((ext_rows, CONV_COLS), F32),
                        pltpu.VMEM((ext_rows, CONV_COLS), F32)],
        compiler_params=_params("parallel", "parallel", "arbitrary"),
    )(q, k, vt, cg, cg, cg, u, u, u, wscc, wcfc)


def _attention_any(scores_bounded, *operands, tq):
    return lax.cond(scores_bounded,
                    functools.partial(_attention, tq=tq, bounded=True),
                    functools.partial(_attention, tq=tq, bounded=False),
                    *operands)


def _mix_kernel(x_ref, attn_ref, bg_ref, cc_ref, cu_ref, gate_ref, mod_ref,
                wao_ref, wsc_ref, wcf_ref, wo_ref, bcfc_ref, gln_ref, bln_ref, bcfo_ref, o_ref):
    dot = functools.partial(jnp.dot, preferred_element_type=F32)
    tm = x_ref.shape[1]
    sub = min(SUB_ROWS, tm)
    for r0 in range(0, tm, sub):
        rows = slice(r0, r0 + sub)
        attn = jnp.concatenate([attn_ref[0, hd, rows, :] for hd in range(N_Q_HEADS)], axis=-1)
        y_attn = dot(attn, wao_ref[...])
        y_sc = dot(bg_ref[0, rows, :] * cc_ref[0, rows, :], wsc_ref[...])

        uc = cu_ref[0, rows, :].astype(F32) + bcfc_ref[...]
        mu = jnp.mean(uc, axis=-1, keepdims=True)
        xc = uc - mu
        un = xc * lax.rsqrt(jnp.mean(xc * xc, axis=-1, keepdims=True) + LN_EPS)
        un = un * gln_ref[...] + bln_ref[...]
        un = un * _sigmoid(un)
        y_cf = dot(un.astype(BF16), wcf_ref[...]) + bcfo_ref[...]

        ga = gate_ref[0, rows, 0:D_MODEL].astype(F32)
        gb = gate_ref[0, rows, D_MODEL:2 * D_MODEL].astype(F32)
        gc = gate_ref[0, rows, 2 * D_MODEL:3 * D_MODEL].astype(F32)
        merged = ga * y_attn + gb * y_sc + gc * y_cf
        y = dot(merged.astype(BF16), wo_ref[...])
        gt1 = mod_ref[0, :, 2 * D_MODEL:3 * D_MODEL]
        o_ref[0, rows, :] = x_ref[0, rows, :] + gt1 * y


def _mixer(x, attn, bg, cc, cu, gates, mod, lw, *, tm):
    bsz, seq, _ = x.shape
    tm = _row_tile(seq, tm)
    tok = lambda width: pl.BlockSpec((1, tm, width), lambda b, i: (b, i, 0))
    row = _resident((1, D_MODEL))
    sq = _resident((D_MODEL, D_MODEL))
    return pl.pallas_call(
        _mix_kernel,
        grid=(bsz, seq // tm),
        in_specs=[
            tok(D_MODEL),
            pl.BlockSpec((1, N_Q_HEADS, tm, HEAD_DIM), lambda b, i: (b, 0, i, 0)),
            tok(D_MODEL), tok(D_MODEL), tok(D_MODEL), tok(3 * D_MODEL),
            _mod_spec(mod),
            sq, sq, sq, sq,
            row, row, row, row,
        ],
        out_specs=tok(D_MODEL),
        out_shape=jax.ShapeDtypeStruct((bsz, seq, D_MODEL), F32),
        compiler_params=_params("parallel", "parallel"),
    )(x, attn, bg, cc, cu, gates, mod,
      lw["w_attn_out"], lw["w_sc_out"], lw["w_cf_out"], lw["w_o"],
      lw["b_cf_conv"], lw["g_cf_ln"], lw["b_cf_ln"], lw["b_cf_out"])


def _mlp_kernel(x_ref, mod_ref, g_ref, w1_ref, w2_ref, gf_ref, o_ref, *, final_norm):
    tm = x_ref.shape[1]
    sub = min(SUB_ROWS, tm)
    for r0 in range(0, tm, sub):
        rows = slice(r0, r0 + sub)
        xt = x_ref[0, rows, :]
        h = _modulated_rmsnorm(xt, g_ref[...], mod_ref[0, :, 3 * D_MODEL:4 * D_MODEL],
                               mod_ref[0, :, 4 * D_MODEL:5 * D_MODEL])
        hb = h.astype(BF16)
        acc = jnp.zeros(xt.shape, F32)
        for j in range(D_FF // D_MODEL):
            sl = slice(j * D_MODEL, (j + 1) * D_MODEL)
            hid = jnp.dot(hb, w1_ref[:, sl], preferred_element_type=F32)
            hid = jnp.square(jnp.maximum(hid, 0.0)).astype(BF16)
            acc = acc + jnp.dot(hid, w2_ref[sl, :], preferred_element_type=F32)
        y = xt + mod_ref[0, :, 5 * D_MODEL:6 * D_MODEL] * acc
        if final_norm:
            ms = jnp.mean(y * y, axis=-1, keepdims=True)
            y = y * lax.rsqrt(ms + NORM_EPS) * gf_ref[...]
        o_ref[0, rows, :] = y


def _mlp(x, mod, g_norm, w1, w2, g_final, *, final_norm, tm):
    bsz, seq, _ = x.shape
    tm = _row_tile(seq, tm)
    tok = pl.BlockSpec((1, tm, D_MODEL), lambda b, i: (b, i, 0))
    return pl.pallas_call(
        functools.partial(_mlp_kernel, final_norm=final_norm),
        grid=(bsz, seq // tm),
        in_specs=[
            tok,
            _mod_spec(mod),
            _resident((1, D_MODEL)),
            _resident((D_MODEL, D_FF)),
            _resident((D_FF, D_MODEL)),
            _resident((1, D_MODEL)),
        ],
        out_specs=tok,
        out_shape=jax.ShapeDtypeStruct((bsz, seq, D_MODEL), F32),
        compiler_params=_params("parallel", "parallel"),
    )(x, mod, g_norm, w1, w2, g_final)


def _rope_tables(n_tokens):
    rows = n_tokens // GRID_W
    row = jnp.repeat(jnp.arange(rows), GRID_W)
    col = jnp.tile(jnp.arange(GRID_W), rows)
    pos = jnp.stack([row, col], axis=-1).astype(F32)
    inv_freq = ROPE_THETA ** (-jnp.arange(ROPE_FREQS, dtype=F32) * 2.0 / ROPE_AXIS_DIM)
    ang = pos[:, :, None] * inv_freq
    cos, sin = jnp.cos(ang), jnp.sin(ang)
    cos_t = jnp.concatenate([cos, cos], axis=-1).reshape(n_tokens, HEAD_DIM)
    sin_t = jnp.concatenate([-sin, sin], axis=-1).reshape(n_tokens, HEAD_DIM)
    return cos_t, sin_t


def kernel(x, c, ctx, c_ctx, w_mod, b_mod, g_norm1, g_norm2, w_in, q_gain, k_gain, w_attn_out,
           w_sc_conv, w_sc_out, w_cf_conv, b_cf_conv, g_cf_ln, b_cf_ln, w_cf_out, b_cf_out,
           w_o, w_mlp_in, w_mlp_out, g_final):
    bsz, seq, _ = x.shape
    n_ctx = ctx.shape[1]
    depth = w_mod.shape[0]
    assert bsz + 1 <= MOD_ROWS and seq % GRID_W == 0
    row = lambda v: v.reshape(1, -1)
    taps = lambda w: jnp.broadcast_to(w[:, None, :], (w.shape[0], SUBLANES, D_MODEL))

    cond = jnp.zeros((MOD_ROWS, D_MODEL), F32).at[:bsz].set(c).at[bsz].set(c_ctx)
    mod_all = _modulation(cond, w_mod, b_mod)
    cos_lat, sin_lat = _rope_tables(seq)
    cos_ctx, sin_ctx = cos_lat[:n_ctx], sin_lat[:n_ctx]

    for l in range(depth):
        last = l == depth - 1
        mod_lat = mod_all[l, :bsz][:, None, :]
        mod_ctx = mod_all[l, bsz:bsz + 1][:, None, :]
        w_in_l = w_in[l].astype(BF16)
        lw = dict(
            w_attn_out=w_attn_out[l].astype(BF16), w_sc_out=w_sc_out[l].astype(BF16),
            w_cf_out=w_cf_out[l].astype(BF16), w_o=w_o[l].astype(BF16),
            b_cf_conv=row(b_cf_conv[l]), g_cf_ln=row(g_cf_ln[l]), b_cf_ln=row(b_cf_ln[l]),
            b_cf_out=row(b_cf_out[l]))
        conv_w = (taps(w_sc_conv[l]), taps(w_cf_conv[l]))
        w1 = w_mlp_in[l].astype(BF16)
        w2 = w_mlp_out[l].astype(BF16)
        gains = (row(q_gain[l]), row(k_gain[l]))
        score_bound = HEAD_DIM ** 0.5 * jnp.max(jnp.abs(q_gain[l])) * jnp.max(jnp.abs(k_gain[l]))
        bounded = score_bound <= SAFE_SCORE_BOUND

        if last:
            k_ctx, vt_ctx = _inproj(ctx, mod_ctx, row(g_norm1[l]), w_in_l[:, OFF_K:OFF_SC], *gains,
                                    cos_ctx, sin_ctx, rope=False, kv_only=True, tm=256)
        else:
            q_c, k_ctx, vt_ctx, bg_c, cg_c, u_c, gates_c = _inproj(
                ctx, mod_ctx, row(g_norm1[l]), w_in_l, *gains, cos_ctx, sin_ctx,
                rope=False, kv_only=False, tm=256)

        q, k_lat, vt_lat, bg, cg, u, gates = _inproj(
            x, mod_lat, row(g_norm1[l]), w_in_l, *gains, cos_lat, sin_lat,
            rope=True, kv_only=False, tm=512)
        k_all = jnp.concatenate([k_ctx, k_lat], axis=1)
        vt_all = jnp.concatenate([vt_ctx, vt_lat], axis=2)
        attn, cc, cu = _attention_any(bounded, q, k_all, vt_all, cg, u, *conv_w, tq=512)
        x = _mixer(x, attn, bg, cc, cu, gates, mod_lat, lw, tm=512)
        x = _mlp(x, mod_lat, row(g_norm2[l]), w1, w2, row(g_final), final_norm=last, tm=1024)

        if not last:
            attn_c, cc_c, cu_c = _attention_any(bounded, q_c, k_ctx, vt_ctx, cg_c, u_c, *conv_w,
                                                tq=256)
            ctx = _mixer(ctx, attn_c, bg_c, cc_c, cu_c, gates_c, mod_ctx, lw, tm=256)
            ctx = _mlp(ctx, mod_ctx, row(g_norm2[l]), w1, w2, row(g_final), final_norm=False, tm=256)
    return x
```
